```python
import math
import jax, jax.numpy as jnp
from jax import lax
import numpy as np

D_MODEL = 1024
BATCH = 2
SEQ = 8192
DEPTH = 1

D_FF = 2816
D_CONV = D_MODEL
CONV_WIDTH = 31
HEAD_DIM = 64
N_HEADS = D_MODEL // HEAD_DIM
N_KV_HEADS = 4
GROUP = N_HEADS // N_KV_HEADS
WINDOW = 128
ROPE_THETA = 10000.0
EPS = 1e-6
LN_EPS = 1e-5
NEG_INF = -1e30

SPLITS = (D_CONV, D_CONV, N_HEADS * HEAD_DIM, N_KV_HEADS * HEAD_DIM, N_KV_HEADS * HEAD_DIM, D_MODEL, D_MODEL)
D_IN = sum(SPLITS)

kernel_name = "hybrid_macaron_conv_swa_gated_block"


def rmsnorm(x, g):
    xf = x.astype(jnp.float32)
    y = xf * lax.rsqrt(jnp.mean(xf * xf, axis=-1, keepdims=True) + EPS)
    return (y * g.astype(jnp.float32)).astype(x.dtype)


def layernorm(x, g, b):
    xf = x.astype(jnp.float32)
    mu = jnp.mean(xf, axis=-1, keepdims=True)
    var = jnp.mean(jnp.square(xf - mu), axis=-1, keepdims=True)
    y = (xf - mu) * lax.rsqrt(var + LN_EPS)
    return (y * g.astype(jnp.float32) + b.astype(jnp.float32)).astype(x.dtype)


def swiglu(x, w_gate, w_up, w_down):
    return (jax.nn.silu(x @ w_gate) * (x @ w_up)) @ w_down


def rope(x, positions):
    half = HEAD_DIM // 2
    inv_freq = ROPE_THETA ** (-jnp.arange(half, dtype=jnp.float32) / half)
    ang = positions.astype(jnp.float32)[..., None] * inv_freq
    cos = jnp.cos(ang)[:, :, None, :]
    sin = jnp.sin(ang)[:, :, None, :]
    xf = x.astype(jnp.float32)
    x1, x2 = xf[..., :half], xf[..., half:]
    out = jnp.concatenate([x1 * cos - x2 * sin, x2 * cos + x1 * sin], axis=-1)
    return out.astype(x.dtype)


def causal_depthwise_conv(u, w, b):
    out = lax.conv_general_dilated(
        u, w[:, None, :].astype(u.dtype), window_strides=(1,),
        padding=((CONV_WIDTH - 1, 0),),
        dimension_numbers=("NWC", "WIO", "NWC"),
        feature_group_count=u.shape[-1])
    return out + b.astype(u.dtype)


def conformer_conv_branch(glu_a, glu_b, dw_w, dw_b, ln_g, ln_b, w_proj):
    u = glu_a * jax.nn.sigmoid(glu_b)
    u = causal_depthwise_conv(u, dw_w, dw_b)
    u = jax.nn.silu(layernorm(u, ln_g, ln_b))
    return u @ w_proj


def band(t):
    B, S = t.shape[:2]
    nb = S // WINDOW
    tb = t.reshape(B, nb, WINDOW, N_KV_HEADS, HEAD_DIM)
    prev = jnp.pad(tb[:, :-1], ((0, 0), (1, 0), (0, 0), (0, 0), (0, 0)))
    return jnp.concatenate([prev, tb], axis=2)


def sliding_window_gqa_sinks(q, k, v, sinks):
    B, S = q.shape[:2]
    nb = S // WINDOW
    qb = q.reshape(B, nb, WINDOW, N_KV_HEADS, GROUP, HEAD_DIM)
    kb, vb = band(k), band(v)
    scores = jnp.einsum("bnqkgd,bnskd->bnkgqs", qb, kb).astype(jnp.float32) * (HEAD_DIM ** -0.5)
    qi = jnp.arange(WINDOW)[:, None]
    sj = jnp.arange(2 * WINDOW)[None, :] - WINDOW
    rel = qi - sj
    allowed = (rel >= 0) & (rel < WINDOW)
    blk = jnp.arange(nb)[:, None, None]
    allowed = allowed[None] & ((blk > 0) | (sj[None] >= 0))
    scores = jnp.where(allowed[None, :, None, None], scores, NEG_INF)
    sink = sinks.astype(jnp.float32).reshape(N_KV_HEADS, GROUP)[None, None, :, :, None, None]
    m = jnp.maximum(jnp.max(scores, axis=-1, keepdims=True), sink)
    p = jnp.exp(scores - m)
    probs = p / (jnp.sum(p, axis=-1, keepdims=True) + jnp.exp(sink - m))
    out = jnp.einsum("bnkgqs,bnskd->bnqkgd", probs.astype(v.dtype), vb)
    return out.reshape(B, S, N_HEADS * HEAD_DIM)


def setup_inputs(seed: int = 0) -> dict:
    key = jax.random.key(seed)
    ks = jax.random.split(key, 24)
    f32 = jnp.float32

    def w(k, shape, fan_in):
        return jax.random.normal(k, shape, f32) * (fan_in ** -0.5)

    def gain(k, n):
        return 1.0 + 0.02 * jax.random.normal(k, (DEPTH, n), f32)

    def small(k, shape):
        return 0.02 * jax.random.normal(k, shape, f32)

    x = jax.random.normal(ks[0], (BATCH, SEQ, D_MODEL), f32)
    positions = jnp.broadcast_to(jnp.arange(SEQ, dtype=jnp.int32)[None, :], (BATCH, SEQ))
    return {
        "x": x,
        "positions": positions,
        "ffn1_norm": gain(ks[1], D_MODEL),
        "ffn1_w_gate": w(ks[2], (DEPTH, D_MODEL, D_FF), D_MODEL),
        "ffn1_w_up": w(ks[3], (DEPTH, D_MODEL, D_FF), D_MODEL),
        "ffn1_w_down": w(ks[4], (DEPTH, D_FF, D_MODEL), D_FF),
        "mix_norm": gain(ks[5], D_MODEL),
        "w_in": w(ks[6], (DEPTH, D_MODEL, D_IN), D_MODEL),
        "conv_dw_w": w(ks[7], (DEPTH, CONV_WIDTH, D_CONV), CONV_WIDTH),
        "conv_dw_b": small(ks[8], (DEPTH, D_CONV)),
        "conv_ln_g": gain(ks[9], D_CONV),
        "conv_ln_b": small(ks[10], (DEPTH, D_CONV)),
        "conv_w_proj": w(ks[11], (DEPTH, D_CONV, D_MODEL), D_CONV),
        "attn_sinks": 0.5 * jax.random.normal(ks[12], (DEPTH, N_HEADS), f32),
        "attn_w_o": w(ks[13], (DEPTH, N_HEADS * HEAD_DIM, D_MODEL), N_HEADS * HEAD_DIM),
        "gate_b": small(ks[14], (DEPTH, 2 * D_MODEL)),
        "w_out": w(ks[15], (DEPTH, D_MODEL, D_MODEL), D_MODEL),
        "ffn2_norm": gain(ks[16], D_MODEL),
        "ffn2_w_gate": w(ks[17], (DEPTH, D_MODEL, D_FF), D_MODEL),
        "ffn2_w_up": w(ks[18], (DEPTH, D_MODEL, D_FF), D_MODEL),
        "ffn2_w_down": w(ks[19], (DEPTH, D_FF, D_MODEL), D_FF),
        "final_norm": 1.0 + 0.02 * jax.random.normal(ks[20], (D_MODEL,), f32),
    }


def reference(x, positions, ffn1_norm, ffn1_w_gate, ffn1_w_up, ffn1_w_down, mix_norm, w_in,
              conv_dw_w, conv_dw_b, conv_ln_g, conv_ln_b, conv_w_proj, attn_sinks, attn_w_o,
              gate_b, w_out, ffn2_norm, ffn2_w_gate, ffn2_w_up, ffn2_w_down, final_norm):
    B, S, _ = x.shape
    bounds = np.cumsum(SPLITS)[:-1].tolist()
    for l in range(DEPTH):
        x = x + 0.5 * swiglu(rmsnorm(x, ffn1_norm[l]), ffn1_w_gate[l], ffn1_w_up[l], ffn1_w_down[l])

        h = rmsnorm(x, mix_norm[l])
        proj = h @ w_in[l]
        glu_a, glu_b, q, k, v, g_conv, g_attn = jnp.split(proj, bounds, axis=-1)

        conv_out = conformer_conv_branch(glu_a, glu_b, conv_dw_w[l], conv_dw_b[l],
                                         conv_ln_g[l], conv_ln_b[l], conv_w_proj[l])

        q = rope(q.reshape(B, S, N_HEADS, HEAD_DIM), positions)
        k = rope(k.reshape(B, S, N_KV_HEADS, HEAD_DIM), positions)
        v = v.reshape(B, S, N_KV_HEADS, HEAD_DIM)
        attn_out = sliding_window_gqa_sinks(q, k, v, attn_sinks[l]) @ attn_w_o[l]

        gb_conv, gb_attn = jnp.split(gate_b[l], 2)
        merged = (jax.nn.sigmoid(g_conv + gb_conv) * conv_out
                  + jax.nn.sigmoid(g_attn + gb_attn) * attn_out)
        x = x + merged @ w_out[l]

        x = x + 0.5 * swiglu(rmsnorm(x, ffn2_norm[l]), ffn2_w_gate[l], ffn2_w_up[l], ffn2_w_down[l])
    return rmsnorm(x, final_norm)
```

```python
import functools
import math

import jax
import jax.numpy as jnp
from jax import lax
from jax.experimental import pallas as pl
from jax.experimental.pallas import tpu as pltpu

D_MODEL = 1024
D_FF = 2816
D_CONV = D_MODEL
CONV_WIDTH = 31
HEAD_DIM = 64
HALF = HEAD_DIM // 2
N_HEADS = D_MODEL // HEAD_DIM
N_KV_HEADS = 4
GROUP = N_HEADS // N_KV_HEADS
D_KV = N_KV_HEADS * HEAD_DIM
WINDOW = 128
ROPE_THETA = 10000.0
EPS = 1e-6
LN_EPS = 1e-5
NEG_INF = -1e30

_SPLITS = (D_CONV, D_CONV, N_HEADS * HEAD_DIM, D_KV, D_KV, D_MODEL, D_MODEL)
_OFF = tuple(sum(_SPLITS[:i]) for i in range(len(_SPLITS) + 1))
D_IN = _OFF[-1]

LANES = 128
SUBLANES = 8
MXU_TILE = 256
V7X_VMEM_BYTES = 64 * 1024 * 1024

FFN_ROWS = 512
MIX_ROWS = 512
CONV_HALO = 32
CONV_CHUNK = 64
FF_CHUNK = 4 * MXU_TILE

F32 = jnp.float32
BF16 = jnp.bfloat16


def _rms(x, g):
    return x * lax.rsqrt(jnp.mean(x * x, axis=-1, keepdims=True) + EPS) * g


def _sigmoid(x):
    return 1.0 / (1.0 + jnp.exp(-x))


def _dot(a, b):
    return jnp.dot(a, b, preferred_element_type=F32)


def _ffn_body(x_ref, g_ref, wg_ref, wu_ref, wd_ref, fg_ref, o_ref, *, final_norm):
    x = x_ref[...]
    h = _rms(x, g_ref[...]).astype(BF16)
    acc = None
    for c0 in range(0, D_FF, FF_CHUNK):
        c1 = min(c0 + FF_CHUNK, D_FF)
        gate = _dot(h, wg_ref[:, c0:c1])
        up = _dot(h, wu_ref[:, c0:c1])
        a = (gate * _sigmoid(gate) * up).astype(BF16)
        y = _dot(a, wd_ref[c0:c1, :])
        acc = y if acc is None else acc + y
    out = x + 0.5 * acc
    if final_norm:
        out = _rms(out, fg_ref[...])
    o_ref[...] = out


def _resident(shape):
    return pl.BlockSpec(shape, lambda *_: (0,) * len(shape), pipeline_mode=pl.Buffered(1))


def _ffn(x, norm_g, w_gate, w_up, w_down, final_g, *, final_norm):
    t = x.shape[0]
    rows = pl.BlockSpec((FFN_ROWS, D_MODEL), lambda i: (i, 0))
    weight_bytes = 3 * D_MODEL * D_FF * 2
    tile_bytes = FFN_ROWS * D_MODEL * 4
    temp_bytes = FFN_ROWS * FF_CHUNK * 4 * 4 + 4 * tile_bytes
    vmem = weight_bytes + 4 * tile_bytes + temp_bytes
    return pl.pallas_call(
        functools.partial(_ffn_body, final_norm=final_norm),
        grid=(t // FFN_ROWS,),
        in_specs=[rows, _resident((1, D_MODEL)), _resident((D_MODEL, D_FF)), _resident((D_MODEL, D_FF)),
                  _resident((D_FF, D_MODEL)), _resident((1, D_MODEL))],
        out_specs=rows,
        out_shape=jax.ShapeDtypeStruct((t, D_MODEL), F32),
        compiler_params=pltpu.CompilerParams(dimension_semantics=("arbitrary",),
                                             vmem_limit_bytes=min(vmem, V7X_VMEM_BYTES)),
        name="ffn_final" if final_norm else "ffn",
    )(x, norm_g, w_gate, w_up, w_down, final_g)


def _mixer_body(x_ref, pos_ref, freq_ref, ng_ref, win_ref, dww_ref, dwb_ref, lng_ref, lnb_ref, wproj_ref,
                sink_ref, wo_ref, gb_ref, wout_ref, o_ref,
                uwin_ref, uhalo_ref, conv_ref, k_ref, v_ref, att_ref):
    j = pl.program_id(1)
    first_tile = j == 0
    n_chunks = MIX_ROWS // CONV_CHUNK

    x = x_ref[...]
    h = _rms(x, ng_ref[...]).astype(BF16)

    def proj(i):
        return _dot(h, win_ref[:, _OFF[i]:_OFF[i + 1]])

    u = proj(0) * _sigmoid(proj(1))

    @pl.when(first_tile)
    def _():
        uhalo_ref[...] = jnp.zeros_like(uhalo_ref)

    uwin_ref[0, 0:CONV_HALO, :] = uhalo_ref[...]
    uwin_ref[0, CONV_HALO:, :] = u[0:CONV_CHUNK]
    for c in range(1, n_chunks):
        uwin_ref[c] = u[c * CONV_CHUNK - CONV_HALO:(c + 1) * CONV_CHUNK]
    uhalo_ref[...] = u[MIX_ROWS - CONV_HALO:]

    tap0 = CONV_HALO - (CONV_WIDTH - 1)

    def conv_chunk(c, carry):
        for l0 in range(0, D_CONV, LANES):
            acc = jnp.broadcast_to(dwb_ref[:, l0:l0 + LANES], (CONV_CHUNK, LANES))
            for k in range(CONV_WIDTH):
                acc = acc + dww_ref[k:k + 1, l0:l0 + LANES] * uwin_ref[c, pl.ds(tap0 + k, CONV_CHUNK), l0:l0 + LANES]
            conv_ref[pl.ds(pl.multiple_of(c * CONV_CHUNK, CONV_CHUNK), CONV_CHUNK), l0:l0 + LANES] = acc
        return carry

    lax.fori_loop(0, n_chunks, conv_chunk, 0)

    cv = conv_ref[...]
    mu = jnp.mean(cv, axis=-1, keepdims=True)
    cen = cv - mu
    var = jnp.mean(cen * cen, axis=-1, keepdims=True)
    ln = cen * lax.rsqrt(var + LN_EPS) * lng_ref[...] + lnb_ref[...]
    conv_out = _dot((ln * _sigmoid(ln)).astype(BF16), wproj_ref[...])

    ang = pos_ref[...].astype(F32) * freq_ref[...]
    cos = jnp.cos(ang)
    sin = jnp.sin(ang)
    lane = lax.broadcasted_iota(jnp.int32, (MIX_ROWS, LANES), 1)
    low_half = (lane % HEAD_DIM) < HALF
    sin_signed = jnp.where(low_half, -sin, sin)

    def rope(t, scale):
        swapped = jnp.where(low_half, pltpu.roll(t, LANES - HALF, 1), pltpu.roll(t, HALF, 1))
        return (t * cos + swapped * sin_signed) * scale

    q = proj(2)
    k = proj(3)
    v = proj(4)
    scale = HEAD_DIM ** -0.5
    q_r = [rope(q[:, l0:l0 + LANES], scale).astype(BF16) for l0 in range(0, N_HEADS * HEAD_DIM, LANES)]
    k_r = [rope(k[:, l0:l0 + LANES], 1.0).astype(BF16) for l0 in range(0, D_KV, LANES)]

    @pl.when(first_tile)
    def _():
        k_ref[0:WINDOW, :] = jnp.zeros((WINDOW, D_KV), BF16)
        v_ref[0:WINDOW, :] = jnp.zeros((WINDOW, D_KV), BF16)

    @pl.when(jnp.logical_not(first_tile))
    def _():
        k_ref[0:WINDOW, :] = k_ref[MIX_ROWS:MIX_ROWS + WINDOW, :]
        v_ref[0:WINDOW, :] = v_ref[MIX_ROWS:MIX_ROWS + WINDOW, :]

    for i, kk in enumerate(k_r):
        k_ref[WINDOW:, i * LANES:(i + 1) * LANES] = kk
    v_ref[WINDOW:, :] = v.astype(BF16)

    qi = lax.broadcasted_iota(jnp.int32, (WINDOW, 2 * WINDOW), 0)
    cj = lax.broadcasted_iota(jnp.int32, (WINDOW, 2 * WINDOW), 1)
    band = (cj > qi) & (cj <= qi + WINDOW)
    band_first = band & ((cj >= WINDOW) | jnp.logical_not(first_tile))

    for n in range(MIX_ROWS // WINDOW):
        allowed = band_first if n == 0 else band
        r0 = n * WINDOW
        for kh in range(N_KV_HEADS):
            kb = k_ref[r0:r0 + 2 * WINDOW, kh * HEAD_DIM:(kh + 1) * HEAD_DIM]
            vb = v_ref[r0:r0 + 2 * WINDOW, kh * HEAD_DIM:(kh + 1) * HEAD_DIM]
            outs = []
            for g in range(GROUP):
                hd = kh * GROUP + g
                qh = q_r[hd // 2][r0:r0 + WINDOW, (hd % 2) * HEAD_DIM:(hd % 2 + 1) * HEAD_DIM]
                s = lax.dot_general(qh, kb, (((1,), (1,)), ((), ())), preferred_element_type=F32)
                s = jnp.where(allowed, s, NEG_INF)
                sink = sink_ref[hd]
                m = jnp.maximum(jnp.max(s, axis=-1, keepdims=True), sink)
                p = jnp.exp(s - m)
                denom = jnp.sum(p, axis=-1, keepdims=True) + jnp.exp(sink - m)
                outs.append(_dot(p.astype(BF16), vb) / denom)
            att_ref[r0:r0 + WINDOW, kh * GROUP * HEAD_DIM:(kh + 1) * GROUP * HEAD_DIM] = (
                jnp.concatenate(outs, axis=1).astype(BF16))

    attn_out = _dot(att_ref[...], wo_ref[...])

    gate_c = _sigmoid(proj(5) + gb_ref[:, 0:D_MODEL])
    gate_a = _sigmoid(proj(6) + gb_ref[:, D_MODEL:2 * D_MODEL])
    merged = (gate_c * conv_out + gate_a * attn_out).astype(BF16)
    o_ref[...] = x + _dot(merged, wout_ref[...])


def _mixer(x, pos, freq, norm_g, w_in, dw_w, dw_b, ln_g, ln_b, w_proj, sinks, w_o, gate_b, w_out, *, batch, seq):
    tiles = seq // MIX_ROWS
    n_chunks = MIX_ROWS // CONV_CHUNK
    rows = pl.BlockSpec((MIX_ROWS, D_MODEL), lambda b, j: (b * tiles + j, 0))
    pos_spec = pl.BlockSpec((MIX_ROWS, 1), lambda b, j: (b * tiles + j, 0))
    weight_bytes = (D_MODEL * D_IN + 3 * D_MODEL * D_MODEL) * 2
    tile_bytes = MIX_ROWS * D_MODEL * 4
    scratch_bytes = (n_chunks * (CONV_CHUNK + CONV_HALO) * D_CONV * 4 + CONV_HALO * D_CONV * 4 + tile_bytes
                     + 2 * (MIX_ROWS + WINDOW) * D_KV * 2 + MIX_ROWS * D_MODEL * 2)
    temp_bytes = 10 * tile_bytes
    vmem = weight_bytes + 4 * tile_bytes + scratch_bytes + temp_bytes
    return pl.pallas_call(
        _mixer_body,
        grid=(batch, tiles),
        in_specs=[rows, pos_spec, _resident((1, LANES)), _resident((1, D_MODEL)), _resident((D_MODEL, D_IN)),
                  _resident((CONV_WIDTH, D_CONV)), _resident((1, D_CONV)), _resident((1, D_CONV)),
                  _resident((1, D_CONV)), _resident((D_CONV, D_MODEL)),
                  pl.BlockSpec(memory_space=pltpu.SMEM),
                  _resident((N_HEADS * HEAD_DIM, D_MODEL)), _resident((1, 2 * D_MODEL)),
                  _resident((D_MODEL, D_MODEL))],
        out_specs=rows,
        out_shape=jax.ShapeDtypeStruct((batch * seq, D_MODEL), F32),
        scratch_shapes=[
            pltpu.VMEM((n_chunks, CONV_CHUNK + CONV_HALO, D_CONV), F32),
            pltpu.VMEM((CONV_HALO, D_CONV), F32),
            pltpu.VMEM((MIX_ROWS, D_CONV), F32),
            pltpu.VMEM((MIX_ROWS + WINDOW, D_KV), BF16),
            pltpu.VMEM((MIX_ROWS + WINDOW, D_KV), BF16),
            pltpu.VMEM((MIX_ROWS, N_HEADS * HEAD_DIM), BF16),
        ],
        compiler_params=pltpu.CompilerParams(dimension_semantics=("arbitrary", "arbitrary"),
                                             vmem_limit_bytes=min(vmem, V7X_VMEM_BYTES)),
        name="mixer",
    )(x, pos, freq, norm_g, w_in, dw_w, dw_b, ln_g, ln_b, w_proj, sinks, w_o, gate_b, w_out)


def kernel(x, positions, ffn1_norm, ffn1_w_gate, ffn1_w_up, ffn1_w_down, mix_norm, w_in, conv_dw_w, conv_dw_b,
           conv_ln_g, conv_ln_b, conv_w_proj, attn_sinks, attn_w_o, gate_b, w_out, ffn2_norm, ffn2_w_gate,
           ffn2_w_up, ffn2_w_down, final_norm):
    batch, seq, d = x.shape
    assert d == D_MODEL and seq % MIX_ROWS == 0 and (batch * seq) % FFN_ROWS == 0 and MIX_ROWS % WINDOW == 0
    depth = ffn1_norm.shape[0]
    t = batch * seq
    xt = x.reshape(t, D_MODEL)
    pos = positions.reshape(t, 1).astype(jnp.int32)
    inv_freq = ROPE_THETA ** (-jnp.arange(HALF, dtype=F32) / HALF)
    freq = jnp.tile(inv_freq, LANES // HALF).reshape(1, LANES)
    row = lambda a: a.reshape(1, -1)
    bf = lambda a: a.astype(BF16)
    fg = row(final_norm)
    for l in range(depth):
        xt = _ffn(xt, row(ffn1_norm[l]), bf(ffn1_w_gate[l]), bf(ffn1_w_up[l]), bf(ffn1_w_down[l]), fg,
                  final_norm=False)
        xt = _mixer(xt, pos, freq, row(mix_norm[l]), bf(w_in[l]), conv_dw_w[l], row(conv_dw_b[l]),
                    row(conv_ln_g[l]), row(conv_ln_b[l]), bf(conv_w_proj[l]), attn_sinks[l], bf(attn_w_o[l]),
                    row(gate_b[l]), bf(w_out[l]), batch=batch, seq=seq)
        xt = _ffn(xt, row(ffn2_norm[l]), bf(ffn2_w_gate[l]), bf(ffn2_w_up[l]), bf(ffn2_w_down[l]), fg,
                  final_norm=(l == depth - 1))
    return xt.reshape(batch, seq, D_MODEL)
```

```python
import functools

import jax
import jax.numpy as jnp
from jax import lax
from jax.experimental import pallas as pl
from jax.experimental.pallas import tpu as pltpu

D_MODEL = 1024
D_FF = 2816
D_CONV = D_MODEL
CONV_WIDTH = 31
HEAD_DIM = 64
HALF = HEAD_DIM // 2
N_HEADS = D_MODEL // HEAD_DIM
N_KV_HEADS = 4
GROUP = N_HEADS // N_KV_HEADS
D_Q = N_HEADS * HEAD_DIM
D_KV = N_KV_HEADS * HEAD_DIM
WINDOW = 128
ROPE_THETA = 10000.0
EPS = 1e-6
LN_EPS = 1e-5
NEG_INF = -1e30

_SPLITS = (D_CONV, D_CONV, D_Q, D_KV, D_KV, D_MODEL, D_MODEL)
_OFF = tuple(sum(_SPLITS[:i]) for i in range(len(_SPLITS) + 1))
D_IN = _OFF[-1]
D_QKV = D_Q + 2 * D_KV

LANES = 128
SUBLANES = 8
MXU_TILE = 256
V7X_VMEM_BYTES = 64 * 1024 * 1024

FFN_ROWS = 512
MIX_ROWS = 512
CONV_HALO = 32
CONV_CHUNK = 64
FF_CHUNK = 4 * MXU_TILE

F32 = jnp.float32
BF16 = jnp.bfloat16

_NT = (((1,), (1,)), ((), ()))
_TN = (((0,), (0,)), ((), ()))


def _rms(x, g):
    return x * lax.rsqrt(jnp.mean(x * x, axis=-1, keepdims=True) + EPS) * g


def _sigmoid(x):
    return 1.0 / (1.0 + jnp.exp(-x))


def _dot(a, b, dims=None):
    if dims is None:
        return jnp.dot(a, b, preferred_element_type=F32)
    return lax.dot_general(a, b, dims, preferred_element_type=F32)


def _ffn_body(x_ref, g_ref, wg_ref, wu_ref, wd_ref, fg_ref, o_ref, *, final_norm):
    x = x_ref[...]
    h = _rms(x, g_ref[...]).astype(BF16)
    acc = None
    for c0 in range(0, D_FF, FF_CHUNK):
        c1 = min(c0 + FF_CHUNK, D_FF)
        gate = _dot(h, wg_ref[:, c0:c1])
        up = _dot(h, wu_ref[:, c0:c1])
        a = (gate * _sigmoid(gate) * up).astype(BF16)
        y = _dot(a, wd_ref[c0:c1, :])
        acc = y if acc is None else acc + y
    out = x + 0.5 * acc
    if final_norm:
        out = _rms(out, fg_ref[...])
    o_ref[...] = out


def _resident(shape):
    return pl.BlockSpec(shape, lambda *_: (0,) * len(shape), pipeline_mode=pl.Buffered(1))


def _ffn(x, norm_g, w_gate, w_up, w_down, final_g, *, final_norm):
    t = x.shape[0]
    rows = pl.BlockSpec((FFN_ROWS, D_MODEL), lambda i: (i, 0))
    weight_bytes = 3 * D_MODEL * D_FF * 2
    tile_bytes = FFN_ROWS * D_MODEL * 4
    temp_bytes = FFN_ROWS * FF_CHUNK * 4 * 4 + 4 * tile_bytes
    vmem = weight_bytes + 4 * tile_bytes + temp_bytes
    return pl.pallas_call(
        functools.partial(_ffn_body, final_norm=final_norm),
        grid=(t // FFN_ROWS,),
        in_specs=[rows, _resident((1, D_MODEL)), _resident((D_MODEL, D_FF)), _resident((D_MODEL, D_FF)),
                  _resident((D_FF, D_MODEL)), _resident((1, D_MODEL))],
        out_specs=rows,
        out_shape=jax.ShapeDtypeStruct((t, D_MODEL), F32),
        compiler_params=pltpu.CompilerParams(dimension_semantics=("arbitrary",),
                                             vmem_limit_bytes=min(vmem, V7X_VMEM_BYTES)),
        name="ffn_final" if final_norm else "ffn",
    )(x, norm_g, w_gate, w_up, w_down, final_g)


def _causal_conv_chunk(c0, u_ref, dww_ref, dwb_ref, conv_ref):
    span = CONV_CHUNK + CONV_HALO
    for l0 in range(0, D_CONV, LANES):
        win = u_ref[c0:c0 + span, l0:l0 + LANES]
        acc = jnp.broadcast_to(dwb_ref[:, l0:l0 + LANES], (CONV_CHUNK, LANES))
        for r in range(SUBLANES):
            rolled = pltpu.roll(win, r, 0) if r else win
            for a in range((CONV_WIDTH - 1 - r) // SUBLANES + 1):
                k = CONV_WIDTH - 1 - (SUBLANES * a + r)
                m0 = CONV_HALO - SUBLANES * a
                acc = acc + dww_ref[k:k + 1, l0:l0 + LANES] * rolled[m0:m0 + CONV_CHUNK]
        conv_ref[c0:c0 + CONV_CHUNK, l0:l0 + LANES] = acc


def _mixer_body(x_ref, pos_ref, freq_ref, ng_ref, win_ref, wqkv_ref, dww_ref, dwb_ref, lng_ref, lnb_ref,
                wproj_ref, sink_ref, wo_ref, gb_ref, wout_ref, o_ref,
                u_ref, conv_ref, qt_ref, kt_ref, vt_ref, at_ref):
    first_tile = pl.program_id(1) == 0

    x = x_ref[...]
    h = _rms(x, ng_ref[...]).astype(BF16)

    def proj(i):
        return _dot(h, win_ref[:, _OFF[i]:_OFF[i + 1]])

    @pl.when(first_tile)
    def _():
        u_ref[0:CONV_HALO, :] = jnp.zeros((CONV_HALO, D_CONV), F32)
        kt_ref[:, 0:WINDOW] = jnp.zeros((D_KV, WINDOW), BF16)
        vt_ref[:, 0:WINDOW] = jnp.zeros((D_KV, WINDOW), BF16)

    @pl.when(jnp.logical_not(first_tile))
    def _():
        u_ref[0:CONV_HALO, :] = u_ref[MIX_ROWS:MIX_ROWS + CONV_HALO, :]
        kt_ref[:, 0:WINDOW] = kt_ref[:, MIX_ROWS:MIX_ROWS + WINDOW]
        vt_ref[:, 0:WINDOW] = vt_ref[:, MIX_ROWS:MIX_ROWS + WINDOW]

    u_ref[CONV_HALO:, :] = proj(0) * _sigmoid(proj(1))

    qkv_t = _dot(wqkv_ref[...], h, _NT)
    ang = freq_ref[...] * pos_ref[...].astype(F32)
    cos = jnp.cos(ang)
    sin = jnp.sin(ang)

    def rope_head(r0, c, s):
        x1 = qkv_t[r0:r0 + HALF]
        x2 = qkv_t[r0 + HALF:r0 + HEAD_DIM]
        return (x1 * c - x2 * s).astype(BF16), (x2 * c + x1 * s).astype(BF16)

    q_scale = HEAD_DIM ** -0.5
    cos_q, sin_q = cos * q_scale, sin * q_scale
    for hd in range(N_HEADS):
        r0 = hd * HEAD_DIM
        qt_ref[r0:r0 + HALF, :], qt_ref[r0 + HALF:r0 + HEAD_DIM, :] = rope_head(r0, cos_q, sin_q)
    for kh in range(N_KV_HEADS):
        r0 = kh * HEAD_DIM
        kt_ref[r0:r0 + HALF, WINDOW:], kt_ref[r0 + HALF:r0 + HEAD_DIM, WINDOW:] = rope_head(D_Q + r0, cos, sin)
    vt_ref[:, WINDOW:] = qkv_t[D_Q + D_KV:D_QKV].astype(BF16)

    gate_c = _sigmoid(proj(5) + gb_ref[:, 0:D_MODEL])
    gate_a = _sigmoid(proj(6) + gb_ref[:, D_MODEL:2 * D_MODEL])

    kc = lax.broadcasted_iota(jnp.int32, (2 * WINDOW, GROUP * WINDOW), 0)
    qq = lax.broadcasted_iota(jnp.int32, (2 * WINDOW, GROUP * WINDOW), 1) % WINDOW
    band = (kc > qq) & (kc <= qq + WINDOW)
    band_first = band & ((kc >= WINDOW) | jnp.logical_not(first_tile))

    def attend(n, kh):
        allowed = band_first if n == 0 else band
        t0 = n * WINDOW
        f0 = kh * HEAD_DIM
        h0 = kh * GROUP * HEAD_DIM
        k_band = kt_ref[f0:f0 + HEAD_DIM, t0:t0 + 2 * WINDOW]
        v_band = vt_ref[f0:f0 + HEAD_DIM, t0:t0 + 2 * WINDOW]
        q_grp = jnp.concatenate(
            [qt_ref[h0 + g * HEAD_DIM:h0 + (g + 1) * HEAD_DIM, t0:t0 + WINDOW] for g in range(GROUP)],
            axis=1)
        s = jnp.where(allowed, _dot(k_band, q_grp, _TN), NEG_INF)
        sink = sink_ref[kh]
        m = jnp.maximum(jnp.max(s, axis=0, keepdims=True), sink)
        p = jnp.exp(s - m)
        inv = 1.0 / (jnp.sum(p, axis=0, keepdims=True) + jnp.exp(sink - m))
        o_t = _dot(v_band, p.astype(BF16)) * inv
        for g in range(GROUP):
            at_ref[h0 + g * HEAD_DIM:h0 + (g + 1) * HEAD_DIM, t0:t0 + WINDOW] = (
                o_t[:, g * WINDOW:(g + 1) * WINDOW].astype(BF16))

    items = [(n, kh) for n in range(MIX_ROWS // WINDOW) for kh in range(N_KV_HEADS)]
    n_chunks = MIX_ROWS // CONV_CHUNK
    per_chunk = -(-len(items) // n_chunks)
    for c in range(n_chunks):
        _causal_conv_chunk(c * CONV_CHUNK, u_ref, dww_ref, dwb_ref, conv_ref)
        for n, kh in items[c * per_chunk:(c + 1) * per_chunk]:
            attend(n, kh)

    cv = conv_ref[...]
    mu = jnp.mean(cv, axis=-1, keepdims=True)
    cen = cv - mu
    var = jnp.mean(cen * cen, axis=-1, keepdims=True)
    ln = cen * lax.rsqrt(var + LN_EPS) * lng_ref[...] + lnb_ref[...]
    conv_out = _dot((ln * _sigmoid(ln)).astype(BF16), wproj_ref[...])

    attn_out = _dot(at_ref[...], wo_ref[...], _TN)

    merged = (gate_c * conv_out + gate_a * attn_out).astype(BF16)
    o_ref[...] = x + _dot(merged, wout_ref[...])


def _mixer(x, pos, freq, norm_g, w_in, w_qkv_t, dw_w, dw_b, ln_g, ln_b, w_proj, sinks, w_o, gate_b, w_out, *,
           batch, seq):
    tiles = seq // MIX_ROWS
    rows = pl.BlockSpec((MIX_ROWS, D_MODEL), lambda b, j: (b * tiles + j, 0))
    pos_spec = pl.BlockSpec((None, 1, MIX_ROWS), lambda b, j: (b * tiles + j, 0, 0))
    weight_bytes = (D_MODEL * (D_IN + D_QKV) + 3 * D_MODEL * D_MODEL) * 2
    tile_bytes = MIX_ROWS * D_MODEL * 4
    scratch_bytes = ((MIX_ROWS + CONV_HALO) * D_CONV * 4 + tile_bytes
                     + 2 * D_Q * MIX_ROWS * 2 + 2 * D_KV * (MIX_ROWS + WINDOW) * 2)
    temp_bytes = 12 * tile_bytes
    vmem = weight_bytes + 4 * tile_bytes + scratch_bytes + temp_bytes
    return pl.pallas_call(
        _mixer_body,
        grid=(batch, tiles),
        in_specs=[rows, pos_spec, _resident((HALF, MIX_ROWS)), _resident((1, D_MODEL)),
                  _resident((D_MODEL, D_IN)), _resident((D_QKV, D_MODEL)),
                  _resident((CONV_WIDTH, D_CONV)), _resident((1, D_CONV)), _resident((1, D_CONV)),
                  _resident((1, D_CONV)), _resident((D_CONV, D_MODEL)),
                  _resident((N_KV_HEADS, 1, GROUP * WINDOW)),
                  _resident((D_Q, D_MODEL)), _resident((1, 2 * D_MODEL)), _resident((D_MODEL, D_MODEL))],
        out_specs=rows,
        out_shape=jax.ShapeDtypeStruct((batch * seq, D_MODEL), F32),
        scratch_shapes=[
            pltpu.VMEM((CONV_HALO + MIX_ROWS, D_CONV), F32),
            pltpu.VMEM((MIX_ROWS, D_CONV), F32),
            pltpu.VMEM((D_Q, MIX_ROWS), BF16),
            pltpu.VMEM((D_KV, WINDOW + MIX_ROWS), BF16),
            pltpu.VMEM((D_KV, WINDOW + MIX_ROWS), BF16),
            pltpu.VMEM((D_Q, MIX_ROWS), BF16),
        ],
        compiler_params=pltpu.CompilerParams(dimension_semantics=("arbitrary", "arbitrary"),
                                             vmem_limit_bytes=min(vmem, V7X_VMEM_BYTES)),
        name="mixer",
    )(x, pos, freq, norm_g, w_in, w_qkv_t, dw_w, dw_b, ln_g, ln_b, w_proj, sinks, w_o, gate_b, w_out)


def kernel(x, positions, ffn1_norm, ffn1_w_gate, ffn1_w_up, ffn1_w_down, mix_norm, w_in, conv_dw_w, conv_dw_b,
           conv_ln_g, conv_ln_b, conv_w_proj, attn_sinks, attn_w_o, gate_b, w_out, ffn2_norm, ffn2_w_gate,
           ffn2_w_up, ffn2_w_down, final_norm):
    batch, seq, d = x.shape
    assert d == D_MODEL and seq % MIX_ROWS == 0 and (batch * seq) % FFN_ROWS == 0 and MIX_ROWS % WINDOW == 0
    depth = ffn1_norm.shape[0]
    t = batch * seq
    xt = x.reshape(t, D_MODEL)
    pos = positions.reshape(t // MIX_ROWS, 1, MIX_ROWS).astype(jnp.int32)
    inv_freq = ROPE_THETA ** (-jnp.arange(HALF, dtype=F32) / HALF)
    freq = jnp.broadcast_to(inv_freq[:, None], (HALF, MIX_ROWS))
    row = lambda a: a.reshape(1, -1)
    bf = lambda a: a.astype(BF16)
    fg = row(final_norm)
    for l in range(depth):
        xt = _ffn(xt, row(ffn1_norm[l]), bf(ffn1_w_gate[l]), bf(ffn1_w_up[l]), bf(ffn1_w_down[l]), fg,
                  final_norm=False)
        w_in_l = bf(w_in[l])
        sinks = jnp.repeat(attn_sinks[l].reshape(N_KV_HEADS, GROUP), WINDOW, axis=1).reshape(
            N_KV_HEADS, 1, GROUP * WINDOW)
        xt = _mixer(xt, pos, freq, row(mix_norm[l]), w_in_l, w_in_l[:, _OFF[2]:_OFF[5]].T, conv_dw_w[l],
                    row(conv_dw_b[l]), row(conv_ln_g[l]), row(conv_ln_b[l]), bf(conv_w_proj[l]), sinks,
                    bf(attn_w_o[l]), row(gate_b[l]), bf(w_out[l]), batch=batch, seq=seq)
        xt = _ffn(xt, row(ffn2_norm[l]), bf(ffn2_w_gate[l]), bf(ffn2_w_up[l]), bf(ffn2_w_down[l]), fg,
                  final_norm=(l == depth - 1))
    return xt.reshape(batch, seq, D_MODEL)
```

```python
import functools

import jax
import jax.numpy as jnp
from jax import lax
from jax.experimental import pallas as pl
from jax.experimental.pallas import tpu as pltpu

D_MODEL = 1024
D_FF = 2816
D_CONV = D_MODEL
CONV_WIDTH = 31
HEAD_DIM = 64
HALF = HEAD_DIM // 2
N_HEADS = D_MODEL // HEAD_DIM
N_KV_HEADS = 4
GROUP = N_HEADS // N_KV_HEADS
D_Q = N_HEADS * HEAD_DIM
D_KV = N_KV_HEADS * HEAD_DIM
WINDOW = 128
ROPE_THETA = 10000.0
EPS = 1e-6
LN_EPS = 1e-5
NEG_INF = -1e30

_SPLITS = (D_CONV, D_CONV, D_Q, D_KV, D_KV, D_MODEL, D_MODEL)
_OFF = tuple(sum(_SPLITS[:i]) for i in range(len(_SPLITS) + 1))
D_IN = _OFF[-1]
D_QKV = D_Q + 2 * D_KV

LANES = 128
SUBLANES = 8
MXU_TILE = 256
V7X_VMEM_BYTES = 64 * 1024 * 1024

FFN_ROWS = 512
MIX_ROWS = 512
CONV_HALO = 32
CONV_CHUNK = 64
ROW_STRIDE = 4
FF_CHUNK = 4 * MXU_TILE
N_SLABS = D_CONV // LANES

F32 = jnp.float32
BF16 = jnp.bfloat16

_NT = (((1,), (1,)), ((), ()))
_TN = (((0,), (0,)), ((), ()))


def _rms(x, g):
    return x * lax.rsqrt(jnp.mean(x * x, axis=-1, keepdims=True) + EPS) * g


def _sigmoid(x):
    return 1.0 / (1.0 + jnp.exp(-x))


def _dot(a, b, dims=None):
    if dims is None:
        return jnp.dot(a, b, preferred_element_type=F32)
    return lax.dot_general(a, b, dims, preferred_element_type=F32)


def _ffn_body(x_ref, g_ref, wg_ref, wu_ref, wd_ref, fg_ref, o_ref, *, final_norm):
    x = x_ref[...]
    h = _rms(x, g_ref[...]).astype(BF16)
    acc = None
    for c0 in range(0, D_FF, FF_CHUNK):
        c1 = min(c0 + FF_CHUNK, D_FF)
        gate = _dot(h, wg_ref[:, c0:c1])
        up = _dot(h, wu_ref[:, c0:c1])
        a = (gate * _sigmoid(gate) * up).astype(BF16)
        y = _dot(a, wd_ref[c0:c1, :])
        acc = y if acc is None else acc + y
    out = x + 0.5 * acc
    if final_norm:
        out = _rms(out, fg_ref[...])
    o_ref[...] = out


def _resident(shape):
    return pl.BlockSpec(shape, lambda *_: (0,) * len(shape), pipeline_mode=pl.Buffered(1))


def _ffn(x, norm_g, w_gate, w_up, w_down, final_g, *, final_norm):
    t = x.shape[0]
    rows = pl.BlockSpec((FFN_ROWS, D_MODEL), lambda i: (i, 0))
    weight_bytes = 3 * D_MODEL * D_FF * 2
    tile_bytes = FFN_ROWS * D_MODEL * 4
    temp_bytes = FFN_ROWS * FF_CHUNK * 4 * 4 + 4 * tile_bytes
    vmem = weight_bytes + 4 * tile_bytes + temp_bytes
    return pl.pallas_call(
        functools.partial(_ffn_body, final_norm=final_norm),
        grid=(t // FFN_ROWS,),
        in_specs=[rows, _resident((1, D_MODEL)), _resident((D_MODEL, D_FF)), _resident((D_MODEL, D_FF)),
                  _resident((D_FF, D_MODEL)), _resident((1, D_MODEL))],
        out_specs=rows,
        out_shape=jax.ShapeDtypeStruct((t, D_MODEL), F32),
        compiler_params=pltpu.CompilerParams(dimension_semantics=("arbitrary",),
                                             vmem_limit_bytes=min(vmem, V7X_VMEM_BYTES)),
        name="ffn_final" if final_norm else "ffn",
    )(x, norm_g, w_gate, w_up, w_down, final_g)


def _causal_conv_chunk(c0, u_ref, dww_ref, dwb_ref, conv_ref):
    per_reg = SUBLANES * ROW_STRIDE
    starts = [c0 + b0 + j for b0 in range(0, CONV_CHUNK, per_reg) for j in range(ROW_STRIDE)]
    for slab in range(N_SLABS):
        l0 = slab * LANES
        accs = [jnp.broadcast_to(dwb_ref[:, l0:l0 + LANES], (SUBLANES, LANES)) for _ in starts]
        for k in range(CONV_WIDTH):
            w_k = dww_ref[k:k + 1, l0:l0 + LANES]
            lag = CONV_WIDTH - 1 - k
            for i, t0 in enumerate(starts):
                tap = u_ref[slab, pl.ds(CONV_HALO + t0 - lag, SUBLANES, stride=ROW_STRIDE), :]
                accs[i] = accs[i] + w_k * tap
        for acc, t0 in zip(accs, starts):
            conv_ref[slab, pl.ds(t0, SUBLANES, stride=ROW_STRIDE), :] = acc


def _mixer_body(x_ref, pos_ref, freq_ref, ng_ref, win_ref, wqkv_ref, dww_ref, dwb_ref, lng_ref, lnb_ref,
                wproj_ref, sink_ref, wo_ref, gb_ref, wout_ref, o_ref,
                u_ref, conv_ref, qt_ref, kt_ref, vt_ref, at_ref):
    first_tile = pl.program_id(1) == 0

    x = x_ref[...]
    h = _rms(x, ng_ref[...]).astype(BF16)

    def proj(i):
        return _dot(h, win_ref[:, _OFF[i]:_OFF[i + 1]])

    @pl.when(first_tile)
    def _():
        u_ref[:, 0:CONV_HALO, :] = jnp.zeros((N_SLABS, CONV_HALO, LANES), F32)
        kt_ref[:, 0:WINDOW] = jnp.zeros((D_KV, WINDOW), BF16)
        vt_ref[:, 0:WINDOW] = jnp.zeros((D_KV, WINDOW), BF16)

    @pl.when(jnp.logical_not(first_tile))
    def _():
        u_ref[:, 0:CONV_HALO, :] = u_ref[:, MIX_ROWS:MIX_ROWS + CONV_HALO, :]
        kt_ref[:, 0:WINDOW] = kt_ref[:, MIX_ROWS:MIX_ROWS + WINDOW]
        vt_ref[:, 0:WINDOW] = vt_ref[:, MIX_ROWS:MIX_ROWS + WINDOW]

    u = proj(0) * _sigmoid(proj(1))
    for slab in range(N_SLABS):
        u_ref[slab, CONV_HALO:, :] = u[:, slab * LANES:(slab + 1) * LANES]

    qkv_t = _dot(wqkv_ref[...], h, _NT)
    ang = freq_ref[...] * pos_ref[...].astype(F32)
    cos = jnp.cos(ang)
    sin = jnp.sin(ang)

    def rope_head(r0, c, s):
        x1 = qkv_t[r0:r0 + HALF]
        x2 = qkv_t[r0 + HALF:r0 + HEAD_DIM]
        return (x1 * c - x2 * s).astype(BF16), (x2 * c + x1 * s).astype(BF16)

    q_scale = HEAD_DIM ** -0.5
    cos_q, sin_q = cos * q_scale, sin * q_scale
    for hd in range(N_HEADS):
        r0 = hd * HEAD_DIM
        qt_ref[r0:r0 + HALF, :], qt_ref[r0 + HALF:r0 + HEAD_DIM, :] = rope_head(r0, cos_q, sin_q)
    for kh in range(N_KV_HEADS):
        r0 = kh * HEAD_DIM
        kt_ref[r0:r0 + HALF, WINDOW:], kt_ref[r0 + HALF:r0 + HEAD_DIM, WINDOW:] = rope_head(D_Q + r0, cos, sin)
    vt_ref[:, WINDOW:] = qkv_t[D_Q + D_KV:D_QKV].astype(BF16)

    gate_c = _sigmoid(proj(5) + gb_ref[:, 0:D_MODEL])
    gate_a = _sigmoid(proj(6) + gb_ref[:, D_MODEL:2 * D_MODEL])

    kc = lax.broadcasted_iota(jnp.int32, (2 * WINDOW, GROUP * WINDOW), 0)
    qq = lax.broadcasted_iota(jnp.int32, (2 * WINDOW, GROUP * WINDOW), 1) % WINDOW
    band = (kc > qq) & (kc <= qq + WINDOW)
    band_first = band & ((kc >= WINDOW) | jnp.logical_not(first_tile))

    def attend(n, kh):
        allowed = band_first if n == 0 else band
        t0 = n * WINDOW
        f0 = kh * HEAD_DIM
        h0 = kh * GROUP * HEAD_DIM
        k_band = kt_ref[f0:f0 + HEAD_DIM, t0:t0 + 2 * WINDOW]
        v_band = vt_ref[f0:f0 + HEAD_DIM, t0:t0 + 2 * WINDOW]
        q_grp = jnp.concatenate(
            [qt_ref[h0 + g * HEAD_DIM:h0 + (g + 1) * HEAD_DIM, t0:t0 + WINDOW] for g in range(GROUP)],
            axis=1)
        s = jnp.where(allowed, _dot(k_band, q_grp, _TN), NEG_INF)
        sink = sink_ref[kh]
        m = jnp.maximum(jnp.max(s, axis=0, keepdims=True), sink)
        p = jnp.exp(s - m)
        inv = 1.0 / (jnp.sum(p, axis=0, keepdims=True) + jnp.exp(sink - m))
        o_t = _dot(v_band, p.astype(BF16)) * inv
        for g in range(GROUP):
            at_ref[h0 + g * HEAD_DIM:h0 + (g + 1) * HEAD_DIM, t0:t0 + WINDOW] = (
                o_t[:, g * WINDOW:(g + 1) * WINDOW].astype(BF16))

    items = [(n, kh) for n in range(MIX_ROWS // WINDOW) for kh in range(N_KV_HEADS)]
    n_chunks = MIX_ROWS // CONV_CHUNK
    per_chunk = -(-len(items) // n_chunks)
    for c in range(n_chunks):
        _causal_conv_chunk(c * CONV_CHUNK, u_ref, dww_ref, dwb_ref, conv_ref)
        for n, kh in items[c * per_chunk:(c + 1) * per_chunk]:
            attend(n, kh)

    cv = jnp.concatenate([conv_ref[slab] for slab in range(N_SLABS)], axis=1)
    mu = jnp.mean(cv, axis=-1, keepdims=True)
    cen = cv - mu
    var = jnp.mean(cen * cen, axis=-1, keepdims=True)
    ln = cen * lax.rsqrt(var + LN_EPS) * lng_ref[...] + lnb_ref[...]
    conv_out = _dot((ln * _sigmoid(ln)).astype(BF16), wproj_ref[...])

    attn_out = _dot(at_ref[...], wo_ref[...], _TN)

    merged = (gate_c * conv_out + gate_a * attn_out).astype(BF16)
    o_ref[...] = x + _dot(merged, wout_ref[...])


def _mixer(x, pos, freq, norm_g, w_in, w_qkv_t, dw_w, dw_b, ln_g, ln_b, w_proj, sinks, w_o, gate_b, w_out, *,
           batch, seq):
    tiles = seq // MIX_ROWS
    rows = pl.BlockSpec((MIX_ROWS, D_MODEL), lambda b, j: (b * tiles + j, 0))
    pos_spec = pl.BlockSpec((None, 1, MIX_ROWS), lambda b, j: (b * tiles + j, 0, 0))
    weight_bytes = (D_MODEL * (D_IN + D_QKV) + 3 * D_MODEL * D_MODEL) * 2
    tile_bytes = MIX_ROWS * D_MODEL * 4
    scratch_bytes = ((MIX_ROWS + CONV_HALO) * D_CONV * 4 + tile_bytes
                     + 2 * D_Q * MIX_ROWS * 2 + 2 * D_KV * (MIX_ROWS + WINDOW) * 2)
    temp_bytes = 12 * tile_bytes
    vmem = weight_bytes + 4 * tile_bytes + scratch_bytes + temp_bytes
    return pl.pallas_call(
        _mixer_body,
        grid=(batch, tiles),
        in_specs=[rows, pos_spec, _resident((HALF, MIX_ROWS)), _resident((1, D_MODEL)),
                  _resident((D_MODEL, D_IN)), _resident((D_QKV, D_MODEL)),
                  _resident((CONV_WIDTH, D_CONV)), _resident((1, D_CONV)), _resident((1, D_CONV)),
                  _resident((1, D_CONV)), _resident((D_CONV, D_MODEL)),
                  _resident((N_KV_HEADS, 1, GROUP * WINDOW)),
                  _resident((D_Q, D_MODEL)), _resident((1, 2 * D_MODEL)), _resident((D_MODEL, D_MODEL))],
        out_specs=rows,
        out_shape=jax.ShapeDtypeStruct((batch * seq, D_MODEL), F32),
        scratch_shapes=[
            pltpu.VMEM((N_SLABS, CONV_HALO + MIX_ROWS, LANES), F32),
            pltpu.VMEM((N_SLABS, MIX_ROWS, LANES), F32),
            pltpu.VMEM((D_Q, MIX_ROWS), BF16),
            pltpu.VMEM((D_KV, WINDOW + MIX_ROWS), BF16),
            pltpu.VMEM((D_KV, WINDOW + MIX_ROWS), BF16),
            pltpu.VMEM((D_Q, MIX_ROWS), BF16),
        ],
        compiler_params=pltpu.CompilerParams(dimension_semantics=("arbitrary", "arbitrary"),
                                             vmem_limit_bytes=min(vmem, V7X_VMEM_BYTES)),
        name="mixer",
    )(x, pos, freq, norm_g, w_in, w_qkv_t, dw_w, dw_b, ln_g, ln_b, w_proj, sinks, w_o, gate_b, w_out)


def kernel(x, positions, ffn1_norm, ffn1_w_gate, ffn1_w_up, ffn1_w_down, mix_norm, w_in, conv_dw_w, conv_dw_b,
           conv_ln_g, conv_ln_b, conv_w_proj, attn_sinks, attn_w_o, gate_b, w_out, ffn2_norm, ffn2_w_gate,
           ffn2_w_up, ffn2_w_down, final_norm):
    batch, seq, d = x.shape
    assert d == D_MODEL and seq % MIX_ROWS == 0 and (batch * seq) % FFN_ROWS == 0 and MIX_ROWS % WINDOW == 0
    depth = ffn1_norm.shape[0]
    t = batch * seq
    xt = x.reshape(t, D_MODEL)
    pos = positions.reshape(t // MIX_ROWS, 1, MIX_ROWS).astype(jnp.int32)
    inv_freq = ROPE_THETA ** (-jnp.arange(HALF, dtype=F32) / HALF)
    freq = jnp.broadcast_to(inv_freq[:, None], (HALF, MIX_ROWS))
    row = lambda a: a.reshape(1, -1)
    bf = lambda a: a.astype(BF16)
    fg = row(final_norm)
    for l in range(depth):
        xt = _ffn(xt, row(ffn1_norm[l]), bf(ffn1_w_gate[l]), bf(ffn1_w_up[l]), bf(ffn1_w_down[l]), fg,
                  final_norm=False)
        w_in_l = bf(w_in[l])
        sinks = jnp.repeat(attn_sinks[l].reshape(N_KV_HEADS, GROUP), WINDOW, axis=1).reshape(
            N_KV_HEADS, 1, GROUP * WINDOW)
        xt = _mixer(xt, pos, freq, row(mix_norm[l]), w_in_l, w_in_l[:, _OFF[2]:_OFF[5]].T, conv_dw_w[l],
                    row(conv_dw_b[l]), row(conv_ln_g[l]), row(conv_ln_b[l]), bf(conv_w_proj[l]), sinks,
                    bf(attn_w_o[l]), row(gate_b[l]), bf(w_out[l]), batch=batch, seq=seq)
        xt = _ffn(xt, row(ffn2_norm[l]), bf(ffn2_w_gate[l]), bf(ffn2_w_up[l]), bf(ffn2_w_down[l]), fg,
                  final_norm=(l == depth - 1))
    return xt.reshape(batch, seq, D_MODEL)
```

```python
import functools

import jax
import jax.numpy as jnp
from jax import lax
from jax.experimental import pallas as pl
from jax.experimental.pallas import tpu as pltpu

D_MODEL = 1024
D_FF = 2816
D_CONV = D_MODEL
CONV_WIDTH = 31
HEAD_DIM = 64
HALF = HEAD_DIM // 2
N_HEADS = D_MODEL // HEAD_DIM
N_KV_HEADS = 4
GROUP = N_HEADS // N_KV_HEADS
D_Q = N_HEADS * HEAD_DIM
D_KV = N_KV_HEADS * HEAD_DIM
WINDOW = 128
ROPE_THETA = 10000.0
EPS = 1e-6
LN_EPS = 1e-5
NEG_INF = -1e30
LOG2E = 1.4426950408889634

_SPLITS = (D_CONV, D_CONV, D_Q, D_KV, D_KV, D_MODEL, D_MODEL)
_OFF = tuple(sum(_SPLITS[:i]) for i in range(len(_SPLITS) + 1))
D_IN = _OFF[-1]
D_QKV = D_Q + 2 * D_KV

LANES = 128
SUBLANES = 8
MXU_TILE = 256
V7X_VMEM_BYTES = 64 * 1024 * 1024

FFN_ROWS = 1024
MIX_ROWS = 512
CONV_HALO = 32
CONV_CHUNK = 64
ROW_STRIDE = 4
FF_CHUNK = 4 * MXU_TILE
N_SLABS = D_CONV // LANES

F32 = jnp.float32
BF16 = jnp.bfloat16

_NT = (((1,), (1,)), ((), ()))
_TN = (((0,), (0,)), ((), ()))


def _rms(x, g):
    return x * lax.rsqrt(jnp.mean(x * x, axis=-1, keepdims=True) + EPS) * g


def _sigmoid(x):
    return 1.0 / (1.0 + jnp.exp2(x * -LOG2E))


def _dot(a, b, dims=None):
    if dims is None:
        return jnp.dot(a, b, preferred_element_type=F32)
    return lax.dot_general(a, b, dims, preferred_element_type=F32)


def _ffn_body(x_ref, g_ref, wg_ref, wu_ref, wd_ref, fg_ref, o_ref, *, final_norm):
    x = x_ref[...]
    h = _rms(x, g_ref[...]).astype(BF16)
    acc = None
    for c0 in range(0, D_FF, FF_CHUNK):
        c1 = min(c0 + FF_CHUNK, D_FF)
        gate = _dot(h, wg_ref[:, c0:c1])
        up = _dot(h, wu_ref[:, c0:c1])
        a = (gate * _sigmoid(gate) * up).astype(BF16)
        y = _dot(a, wd_ref[c0:c1, :])
        acc = y if acc is None else acc + y
    out = x + 0.5 * acc
    if final_norm:
        out = _rms(out, fg_ref[...])
    o_ref[...] = out


def _resident(shape):
    return pl.BlockSpec(shape, lambda *_: (0,) * len(shape), pipeline_mode=pl.Buffered(1))


def _ffn(x, norm_g, w_gate, w_up, w_down, final_g, *, final_norm):
    t = x.shape[0]
    rows = pl.BlockSpec((FFN_ROWS, D_MODEL), lambda i: (i, 0))
    weight_bytes = 3 * D_MODEL * D_FF * 2
    tile_bytes = FFN_ROWS * D_MODEL * 4
    temp_bytes = FFN_ROWS * FF_CHUNK * 4 * 4 + 4 * tile_bytes
    vmem = weight_bytes + 4 * tile_bytes + temp_bytes
    return pl.pallas_call(
        functools.partial(_ffn_body, final_norm=final_norm),
        grid=(t // FFN_ROWS,),
        in_specs=[rows, _resident((1, D_MODEL)), _resident((D_MODEL, D_FF)), _resident((D_MODEL, D_FF)),
                  _resident((D_FF, D_MODEL)), _resident((1, D_MODEL))],
        out_specs=rows,
        out_shape=jax.ShapeDtypeStruct((t, D_MODEL), F32),
        compiler_params=pltpu.CompilerParams(dimension_semantics=("arbitrary",),
                                             vmem_limit_bytes=min(vmem, V7X_VMEM_BYTES)),
        name="ffn_final" if final_norm else "ffn",
    )(x, norm_g, w_gate, w_up, w_down, final_g)


def _causal_conv_chunk(c0, u_ref, dww_ref, dwb_ref, conv_ref):
    per_reg = SUBLANES * ROW_STRIDE
    starts = [c0 + b0 + j for b0 in range(0, CONV_CHUNK, per_reg) for j in range(ROW_STRIDE)]
    for slab in range(N_SLABS):
        l0 = slab * LANES
        accs = [jnp.broadcast_to(dwb_ref[:, l0:l0 + LANES], (SUBLANES, LANES)) for _ in starts]
        for k in range(CONV_WIDTH):
            w_k = dww_ref[k:k + 1, l0:l0 + LANES]
            lag = CONV_WIDTH - 1 - k
            for i, t0 in enumerate(starts):
                tap = u_ref[slab, pl.ds(CONV_HALO + t0 - lag, SUBLANES, stride=ROW_STRIDE), :]
                accs[i] = accs[i] + w_k * tap
        for acc, t0 in zip(accs, starts):
            conv_ref[slab, pl.ds(t0, SUBLANES, stride=ROW_STRIDE), :] = acc


def _mixer_body(x_ref, pos_ref, freq_ref, ng_ref, win_ref, dww_ref, dwb_ref, lng_ref, lnb_ref,
                wproj_ref, sink_ref, wo_ref, gb_ref, wout_ref, o_ref,
                u_ref, conv_ref, qt_ref, kt_ref, vt_ref, at_ref):
    first_tile = pl.program_id(1) == 0

    x = x_ref[...]
    h = _rms(x, ng_ref[...]).astype(BF16)

    def proj(i):
        return _dot(h, win_ref[:, _OFF[i]:_OFF[i + 1]])

    @pl.when(first_tile)
    def _():
        u_ref[:, 0:CONV_HALO, :] = jnp.zeros((N_SLABS, CONV_HALO, LANES), F32)
        kt_ref[:, 0:WINDOW] = jnp.zeros((D_KV, WINDOW), BF16)
        vt_ref[:, 0:WINDOW] = jnp.zeros((D_KV, WINDOW), BF16)

    @pl.when(jnp.logical_not(first_tile))
    def _():
        u_ref[:, 0:CONV_HALO, :] = u_ref[:, MIX_ROWS:MIX_ROWS + CONV_HALO, :]
        kt_ref[:, 0:WINDOW] = kt_ref[:, MIX_ROWS:MIX_ROWS + WINDOW]
        vt_ref[:, 0:WINDOW] = vt_ref[:, MIX_ROWS:MIX_ROWS + WINDOW]

    u = proj(0) * _sigmoid(proj(1))
    for slab in range(N_SLABS):
        u_ref[slab, CONV_HALO:, :] = u[:, slab * LANES:(slab + 1) * LANES]

    qkv_t = _dot(h, win_ref[:, _OFF[2]:_OFF[5]]).T
    ang = freq_ref[...] * pos_ref[...].astype(F32)
    cos = jnp.cos(ang)
    sin = jnp.sin(ang)

    def rope_head(r0, c, s):
        x1 = qkv_t[r0:r0 + HALF]
        x2 = qkv_t[r0 + HALF:r0 + HEAD_DIM]
        return (x1 * c - x2 * s).astype(BF16), (x2 * c + x1 * s).astype(BF16)

    q_scale = HEAD_DIM ** -0.5 * LOG2E
    cos_q, sin_q = cos * q_scale, sin * q_scale
    for hd in range(N_HEADS):
        r0 = hd * HEAD_DIM
        qt_ref[r0:r0 + HALF, :], qt_ref[r0 + HALF:r0 + HEAD_DIM, :] = rope_head(r0, cos_q, sin_q)
    for kh in range(N_KV_HEADS):
        r0 = kh * HEAD_DIM
        kt_ref[r0:r0 + HALF, WINDOW:], kt_ref[r0 + HALF:r0 + HEAD_DIM, WINDOW:] = rope_head(D_Q + r0, cos, sin)
    vt_ref[:, WINDOW:] = qkv_t[D_Q + D_KV:D_QKV].astype(BF16)

    gate_c = _sigmoid(proj(5) + gb_ref[:, 0:D_MODEL])
    gate_a = _sigmoid(proj(6) + gb_ref[:, D_MODEL:2 * D_MODEL])

    kc = lax.broadcasted_iota(jnp.int32, (2 * WINDOW, GROUP * WINDOW), 0)
    qq = lax.broadcasted_iota(jnp.int32, (2 * WINDOW, GROUP * WINDOW), 1) % WINDOW
    band = (kc > qq) & (kc <= qq + WINDOW)
    band_first = band & ((kc >= WINDOW) | jnp.logical_not(first_tile))
    bias = jnp.where(band, 0.0, NEG_INF)
    bias_first = jnp.where(band_first, 0.0, NEG_INF)

    def attend(n, kh):
        mask_bias = bias_first if n == 0 else bias
        t0 = n * WINDOW
        f0 = kh * HEAD_DIM
        h0 = kh * GROUP * HEAD_DIM
        k_band = kt_ref[f0:f0 + HEAD_DIM, t0:t0 + 2 * WINDOW]
        v_band = vt_ref[f0:f0 + HEAD_DIM, t0:t0 + 2 * WINDOW]
        q_grp = jnp.concatenate(
            [qt_ref[h0 + g * HEAD_DIM:h0 + (g + 1) * HEAD_DIM, t0:t0 + WINDOW] for g in range(GROUP)],
            axis=1)
        s = _dot(k_band, q_grp, _TN) + mask_bias
        sink = sink_ref[kh] * LOG2E
        m = jnp.maximum(jnp.max(s, axis=0, keepdims=True), sink)
        p = jnp.exp2(s - m)
        inv = 1.0 / (jnp.sum(p, axis=0, keepdims=True) + jnp.exp2(sink - m))
        o_t = _dot(v_band, p.astype(BF16)) * inv
        for g in range(GROUP):
            at_ref[h0 + g * HEAD_DIM:h0 + (g + 1) * HEAD_DIM, t0:t0 + WINDOW] = (
                o_t[:, g * WINDOW:(g + 1) * WINDOW].astype(BF16))

    items = [(n, kh) for n in range(MIX_ROWS // WINDOW) for kh in range(N_KV_HEADS)]
    n_chunks = MIX_ROWS // CONV_CHUNK
    per_chunk = -(-len(items) // n_chunks)
    for c in range(n_chunks):
        _causal_conv_chunk(c * CONV_CHUNK, u_ref, dww_ref, dwb_ref, conv_ref)
        for n, kh in items[c * per_chunk:(c + 1) * per_chunk]:
            attend(n, kh)

    cv = jnp.concatenate([conv_ref[slab] for slab in range(N_SLABS)], axis=1)
    mu = jnp.mean(cv, axis=-1, keepdims=True)
    cen = cv - mu
    var = jnp.mean(cen * cen, axis=-1, keepdims=True)
    ln = cen * lax.rsqrt(var + LN_EPS) * lng_ref[...] + lnb_ref[...]
    conv_out = _dot((ln * _sigmoid(ln)).astype(BF16), wproj_ref[...])

    attn_out = _dot(at_ref[...], wo_ref[...], _TN)

    merged = (gate_c * conv_out + gate_a * attn_out).astype(BF16)
    o_ref[...] = x + _dot(merged, wout_ref[...])


def _mixer(x, pos, freq, norm_g, w_in, dw_w, dw_b, ln_g, ln_b, w_proj, sinks, w_o, gate_b, w_out, *,
           batch, seq):
    tiles = seq // MIX_ROWS
    rows = pl.BlockSpec((MIX_ROWS, D_MODEL), lambda b, j: (b * tiles + j, 0))
    pos_spec = pl.BlockSpec((None, 1, MIX_ROWS), lambda b, j: (b * tiles + j, 0, 0))
    weight_bytes = (D_MODEL * D_IN + 3 * D_MODEL * D_MODEL) * 2
    tile_bytes = MIX_ROWS * D_MODEL * 4
    scratch_bytes = ((MIX_ROWS + CONV_HALO) * D_CONV * 4 + tile_bytes
                     + 2 * D_Q * MIX_ROWS * 2 + 2 * D_KV * (MIX_ROWS + WINDOW) * 2)
    temp_bytes = 12 * tile_bytes
    vmem = weight_bytes + 4 * tile_bytes + scratch_bytes + temp_bytes
    return pl.pallas_call(
        _mixer_body,
        grid=(batch, tiles),
        in_specs=[rows, pos_spec, _resident((HALF, MIX_ROWS)), _resident((1, D_MODEL)),
                  _resident((D_MODEL, D_IN)),
                  _resident((CONV_WIDTH, D_CONV)), _resident((1, D_CONV)), _resident((1, D_CONV)),
                  _resident((1, D_CONV)), _resident((D_CONV, D_MODEL)),
                  _resident((N_KV_HEADS, 1, GROUP * WINDOW)),
                  _resident((D_Q, D_MODEL)), _resident((1, 2 * D_MODEL)), _resident((D_MODEL, D_MODEL))],
        out_specs=rows,
        out_shape=jax.ShapeDtypeStruct((batch * seq, D_MODEL), F32),
        scratch_shapes=[
            pltpu.VMEM((N_SLABS, CONV_HALO + MIX_ROWS, LANES), F32),
            pltpu.VMEM((N_SLABS, MIX_ROWS, LANES), F32),
            pltpu.VMEM((D_Q, MIX_ROWS), BF16),
            pltpu.VMEM((D_KV, WINDOW + MIX_ROWS), BF16),
            pltpu.VMEM((D_KV, WINDOW + MIX_ROWS), BF16),
            pltpu.VMEM((D_Q, MIX_ROWS), BF16),
        ],
        compiler_params=pltpu.CompilerParams(dimension_semantics=("arbitrary", "arbitrary"),
                                             vmem_limit_bytes=min(vmem, V7X_VMEM_BYTES)),
        name="mixer",
    )(x, pos, freq, norm_g, w_in, dw_w, dw_b, ln_g, ln_b, w_proj, sinks, w_o, gate_b, w_out)


def kernel(x, positions, ffn1_norm, ffn1_w_gate, ffn1_w_up, ffn1_w_down, mix_norm, w_in, conv_dw_w, conv_dw_b,
           conv_ln_g, conv_ln_b, conv_w_proj, attn_sinks, attn_w_o, gate_b, w_out, ffn2_norm, ffn2_w_gate,
           ffn2_w_up, ffn2_w_down, final_norm):
    batch, seq, d = x.shape
    assert d == D_MODEL and seq % MIX_ROWS == 0 and (batch * seq) % FFN_ROWS == 0 and MIX_ROWS % WINDOW == 0
    depth = ffn1_norm.shape[0]
    t = batch * seq
    xt = x.reshape(t, D_MODEL)
    pos = positions.reshape(t // MIX_ROWS, 1, MIX_ROWS).astype(jnp.int32)
    inv_freq = ROPE_THETA ** (-jnp.arange(HALF, dtype=F32) / HALF)
    freq = jnp.broadcast_to(inv_freq[:, None], (HALF, MIX_ROWS))
    row = lambda a: a.reshape(1, -1)
    bf = lambda a: a.astype(BF16)
    fg = row(final_norm)
    for l in range(depth):
        xt = _ffn(xt, row(ffn1_norm[l]), bf(ffn1_w_gate[l]), bf(ffn1_w_up[l]), bf(ffn1_w_down[l]), fg,
                  final_norm=False)
        sinks = jnp.repeat(attn_sinks[l].reshape(N_KV_HEADS, GROUP), WINDOW, axis=1).reshape(
            N_KV_HEADS, 1, GROUP * WINDOW)
        xt = _mixer(xt, pos, freq, row(mix_norm[l]), bf(w_in[l]), conv_dw_w[l],
                    row(conv_dw_b[l]), row(conv_ln_g[l]), row(conv_ln_b[l]), bf(conv_w_proj[l]), sinks,
                    bf(attn_w_o[l]), row(gate_b[l]), bf(w_out[l]), batch=batch, seq=seq)
        xt = _ffn(xt, row(ffn2_norm[l]), bf(ffn2_w_gate[l]), bf(ffn2_w_up[l]), bf(ffn2_w_down[l]), fg,
                  final_norm=(l == depth - 1))
    return xt.reshape(batch, seq, D_MODEL)
```

```python
import functools

import jax
import jax.numpy as jnp
from jax import lax
from jax.experimental import pallas as pl
from jax.experimental.pallas import tpu as pltpu

D_MODEL = 1024
D_FF = 2816
D_CONV = D_MODEL
CONV_WIDTH = 31
HEAD_DIM = 64
HALF = HEAD_DIM // 2
N_HEADS = D_MODEL // HEAD_DIM
N_KV_HEADS = 4
GROUP = N_HEADS // N_KV_HEADS
D_Q = N_HEADS * HEAD_DIM
D_KV = N_KV_HEADS * HEAD_DIM
WINDOW = 128
ROPE_THETA = 10000.0
EPS = 1e-6
LN_EPS = 1e-5
NEG_INF = -1e30
LOG2E = 1.4426950408889634

_SPLITS = (D_CONV, D_CONV, D_Q, D_KV, D_KV, D_MODEL, D_MODEL)
_OFF = tuple(sum(_SPLITS[:i]) for i in range(len(_SPLITS) + 1))
D_IN = _OFF[-1]
D_QKV = D_Q + 2 * D_KV

LANES = 128
SUBLANES = 8
MXU_TILE = 256
V7X_VMEM_BYTES = 64 * 1024 * 1024

FFN_ROWS = 1024
MIX_ROWS = 512
TAIL_ROWS = 256
CONV_HALO = 32
CONV_CHUNK = 64
ROW_STRIDE = 4
FF_CHUNK = 4 * MXU_TILE
N_SLABS = D_CONV // LANES

F32 = jnp.float32
BF16 = jnp.bfloat16

_NT = (((1,), (1,)), ((), ()))
_TN = (((0,), (0,)), ((), ()))


def _rms(x, g):
    return x * lax.rsqrt(jnp.mean(x * x, axis=-1, keepdims=True) + EPS) * g


def _sigmoid(x):
    return 1.0 / (1.0 + jnp.exp2(x * -LOG2E))


def _dot(a, b, dims=None):
    if dims is None:
        return jnp.dot(a, b, preferred_element_type=F32)
    return lax.dot_general(a, b, dims, preferred_element_type=F32)


def _ffn_body(x_ref, g_ref, wg_ref, wu_ref, wd_ref, fg_ref, o_ref, *, final_norm):
    x = x_ref[...]
    h = _rms(x, g_ref[...]).astype(BF16)
    acc = None
    for c0 in range(0, D_FF, FF_CHUNK):
        c1 = min(c0 + FF_CHUNK, D_FF)
        gate = _dot(h, wg_ref[:, c0:c1])
        up = _dot(h, wu_ref[:, c0:c1])
        a = (gate * _sigmoid(gate) * up).astype(BF16)
        y = _dot(a, wd_ref[c0:c1, :])
        acc = y if acc is None else acc + y
    out = x + 0.5 * acc
    if final_norm:
        out = _rms(out, fg_ref[...])
    o_ref[...] = out


def _resident(shape):
    return pl.BlockSpec(shape, lambda *_: (0,) * len(shape), pipeline_mode=pl.Buffered(1))


def _ffn(x, norm_g, w_gate, w_up, w_down, final_g, *, final_norm):
    t = x.shape[0]
    rows = pl.BlockSpec((FFN_ROWS, D_MODEL), lambda i: (i, 0))
    weight_bytes = 3 * D_MODEL * D_FF * 2
    tile_bytes = FFN_ROWS * D_MODEL * 4
    temp_bytes = FFN_ROWS * FF_CHUNK * 4 * 4 + 4 * tile_bytes
    vmem = weight_bytes + 4 * tile_bytes + temp_bytes
    return pl.pallas_call(
        functools.partial(_ffn_body, final_norm=final_norm),
        grid=(t // FFN_ROWS,),
        in_specs=[rows, _resident((1, D_MODEL)), _resident((D_MODEL, D_FF)), _resident((D_MODEL, D_FF)),
                  _resident((D_FF, D_MODEL)), _resident((1, D_MODEL))],
        out_specs=rows,
        out_shape=jax.ShapeDtypeStruct((t, D_MODEL), F32),
        compiler_params=pltpu.CompilerParams(dimension_semantics=("arbitrary",),
                                             vmem_limit_bytes=min(vmem, V7X_VMEM_BYTES)),
        name="ffn_final" if final_norm else "ffn",
    )(x, norm_g, w_gate, w_up, w_down, final_g)


def _causal_conv_chunk(c0, u_ref, dww_ref, dwb_ref, conv_ref):
    per_reg = SUBLANES * ROW_STRIDE
    starts = [c0 + b0 + j for b0 in range(0, CONV_CHUNK, per_reg) for j in range(ROW_STRIDE)]
    for slab in range(N_SLABS):
        l0 = slab * LANES
        accs = [jnp.broadcast_to(dwb_ref[:, l0:l0 + LANES], (SUBLANES, LANES)) for _ in starts]
        for k in range(CONV_WIDTH):
            w_k = dww_ref[k:k + 1, l0:l0 + LANES]
            lag = CONV_WIDTH - 1 - k
            for i, t0 in enumerate(starts):
                tap = u_ref[slab, pl.ds(CONV_HALO + t0 - lag, SUBLANES, stride=ROW_STRIDE), :]
                accs[i] = accs[i] + w_k * tap
        for acc, t0 in zip(accs, starts):
            conv_ref[slab, pl.ds(t0, SUBLANES, stride=ROW_STRIDE), :] = acc


def _mixer_body(x_ref, pos_ref, freq_ref, ng_ref, win_ref, dww_ref, dwb_ref, lng_ref, lnb_ref,
                wproj_ref, sink_ref, wo_ref, gb_ref, wout_ref, o_ref,
                u_ref, conv_ref, qt_ref, kt_ref, vt_ref, at_ref):
    first_tile = pl.program_id(1) == 0

    x = x_ref[...]
    h = _rms(x, ng_ref[...]).astype(BF16)

    def proj(i):
        return _dot(h, win_ref[:, _OFF[i]:_OFF[i + 1]])

    @pl.when(first_tile)
    def _():
        u_ref[:, 0:CONV_HALO, :] = jnp.zeros((N_SLABS, CONV_HALO, LANES), F32)
        kt_ref[:, 0:WINDOW] = jnp.zeros((D_KV, WINDOW), BF16)
        vt_ref[:, 0:WINDOW] = jnp.zeros((D_KV, WINDOW), BF16)

    @pl.when(jnp.logical_not(first_tile))
    def _():
        u_ref[:, 0:CONV_HALO, :] = u_ref[:, MIX_ROWS:MIX_ROWS + CONV_HALO, :]
        kt_ref[:, 0:WINDOW] = kt_ref[:, MIX_ROWS:MIX_ROWS + WINDOW]
        vt_ref[:, 0:WINDOW] = vt_ref[:, MIX_ROWS:MIX_ROWS + WINDOW]

    u = proj(0) * _sigmoid(proj(1))
    for slab in range(N_SLABS):
        u_ref[slab, CONV_HALO:, :] = u[:, slab * LANES:(slab + 1) * LANES]

    qkv_t = _dot(h, win_ref[:, _OFF[2]:_OFF[5]]).T
    ang = freq_ref[...] * pos_ref[...].astype(F32)
    cos = jnp.cos(ang)
    sin = jnp.sin(ang)

    def rope_head(r0, c, s):
        x1 = qkv_t[r0:r0 + HALF]
        x2 = qkv_t[r0 + HALF:r0 + HEAD_DIM]
        return (x1 * c - x2 * s).astype(BF16), (x2 * c + x1 * s).astype(BF16)

    q_scale = HEAD_DIM ** -0.5 * LOG2E
    cos_q, sin_q = cos * q_scale, sin * q_scale
    for hd in range(N_HEADS):
        r0 = hd * HEAD_DIM
        qt_ref[r0:r0 + HALF, :], qt_ref[r0 + HALF:r0 + HEAD_DIM, :] = rope_head(r0, cos_q, sin_q)
    for kh in range(N_KV_HEADS):
        r0 = kh * HEAD_DIM
        kt_ref[r0:r0 + HALF, WINDOW:], kt_ref[r0 + HALF:r0 + HEAD_DIM, WINDOW:] = rope_head(D_Q + r0, cos, sin)
    vt_ref[:, WINDOW:] = qkv_t[D_Q + D_KV:D_QKV].astype(BF16)

    gate_c = _sigmoid(proj(5) + gb_ref[:, 0:D_MODEL])
    gate_a = _sigmoid(proj(6) + gb_ref[:, D_MODEL:2 * D_MODEL])

    kc = lax.broadcasted_iota(jnp.int32, (2 * WINDOW, GROUP * WINDOW), 0)
    qq = lax.broadcasted_iota(jnp.int32, (2 * WINDOW, GROUP * WINDOW), 1) % WINDOW
    band = (kc > qq) & (kc <= qq + WINDOW)
    band_first = band & ((kc >= WINDOW) | jnp.logical_not(first_tile))
    bias = jnp.where(band, 0.0, NEG_INF)
    bias_first = jnp.where(band_first, 0.0, NEG_INF)

    def attend(n, kh):
        mask_bias = bias_first if n == 0 else bias
        t0 = n * WINDOW
        f0 = kh * HEAD_DIM
        h0 = kh * GROUP * HEAD_DIM
        k_band = kt_ref[f0:f0 + HEAD_DIM, t0:t0 + 2 * WINDOW]
        v_band = vt_ref[f0:f0 + HEAD_DIM, t0:t0 + 2 * WINDOW]
        q_grp = jnp.concatenate(
            [qt_ref[h0 + g * HEAD_DIM:h0 + (g + 1) * HEAD_DIM, t0:t0 + WINDOW] for g in range(GROUP)],
            axis=1)
        s = _dot(k_band, q_grp, _TN) + mask_bias
        sink = sink_ref[kh] * LOG2E
        m = jnp.maximum(jnp.max(s, axis=0, keepdims=True), sink)
        p = jnp.exp2(s - m)
        inv = 1.0 / (jnp.sum(p, axis=0, keepdims=True) + jnp.exp2(sink - m))
        o_t = _dot(v_band, p.astype(BF16)) * inv
        for g in range(GROUP):
            at_ref[h0 + g * HEAD_DIM:h0 + (g + 1) * HEAD_DIM, t0:t0 + WINDOW] = (
                o_t[:, g * WINDOW:(g + 1) * WINDOW].astype(BF16))

    for r0 in range(0, MIX_ROWS, TAIL_ROWS):
        rows = slice(r0, r0 + TAIL_ROWS)
        for c0 in range(r0, r0 + TAIL_ROWS, CONV_CHUNK):
            _causal_conv_chunk(c0, u_ref, dww_ref, dwb_ref, conv_ref)
        for n in range(r0 // WINDOW, (r0 + TAIL_ROWS) // WINDOW):
            for kh in range(N_KV_HEADS):
                attend(n, kh)

        cv = jnp.concatenate([conv_ref[slab, rows, :] for slab in range(N_SLABS)], axis=1)
        mu = jnp.mean(cv, axis=-1, keepdims=True)
        cen = cv - mu
        var = jnp.mean(cen * cen, axis=-1, keepdims=True)
        ln = cen * lax.rsqrt(var + LN_EPS) * lng_ref[...] + lnb_ref[...]
        conv_out = _dot((ln * _sigmoid(ln)).astype(BF16), wproj_ref[...])

        attn_out = _dot(at_ref[:, rows], wo_ref[...], _TN)

        merged = (gate_c[rows] * conv_out + gate_a[rows] * attn_out).astype(BF16)
        o_ref[rows, :] = x[rows] + _dot(merged, wout_ref[...])


def _mixer(x, pos, freq, norm_g, w_in, dw_w, dw_b, ln_g, ln_b, w_proj, sinks, w_o, gate_b, w_out, *,
           batch, seq):
    tiles = seq // MIX_ROWS
    rows = pl.BlockSpec((MIX_ROWS, D_MODEL), lambda b, j: (b * tiles + j, 0))
    pos_spec = pl.BlockSpec((None, 1, MIX_ROWS), lambda b, j: (b * tiles + j, 0, 0))
    weight_bytes = (D_MODEL * D_IN + 3 * D_MODEL * D_MODEL) * 2
    tile_bytes = MIX_ROWS * D_MODEL * 4
    scratch_bytes = ((MIX_ROWS + CONV_HALO) * D_CONV * 4 + tile_bytes
                     + 2 * D_Q * MIX_ROWS * 2 + 2 * D_KV * (MIX_ROWS + WINDOW) * 2)
    temp_bytes = 12 * tile_bytes
    vmem = weight_bytes + 4 * tile_bytes + scratch_bytes + temp_bytes
    return pl.pallas_call(
        _mixer_body,
        grid=(batch, tiles),
        in_specs=[rows, pos_spec, _resident((HALF, MIX_ROWS)), _resident((1, D_MODEL)),
                  _resident((D_MODEL, D_IN)),
                  _resident((CONV_WIDTH, D_CONV)), _resident((1, D_CONV)), _resident((1, D_CONV)),
                  _resident((1, D_CONV)), _resident((D_CONV, D_MODEL)),
                  _resident((N_KV_HEADS, 1, GROUP * WINDOW)),
                  _resident((D_Q, D_MODEL)), _resident((1, 2 * D_MODEL)), _resident((D_MODEL, D_MODEL))],
        out_specs=rows,
        out_shape=jax.ShapeDtypeStruct((batch * seq, D_MODEL), F32),
        scratch_shapes=[
            pltpu.VMEM((N_SLABS, CONV_HALO + MIX_ROWS, LANES), F32),
            pltpu.VMEM((N_SLABS, MIX_ROWS, LANES), F32),
            pltpu.VMEM((D_Q, MIX_ROWS), BF16),
            pltpu.VMEM((D_KV, WINDOW + MIX_ROWS), BF16),
            pltpu.VMEM((D_KV, WINDOW + MIX_ROWS), BF16),
            pltpu.VMEM((D_Q, MIX_ROWS), BF16),
        ],
        compiler_params=pltpu.CompilerParams(dimension_semantics=("arbitrary", "arbitrary"),
                                             vmem_limit_bytes=min(vmem, V7X_VMEM_BYTES)),
        name="mixer",
    )(x, pos, freq, norm_g, w_in, dw_w, dw_b, ln_g, ln_b, w_proj, sinks, w_o, gate_b, w_out)


def kernel(x, positions, ffn1_norm, ffn1_w_gate, ffn1_w_up, ffn1_w_down, mix_norm, w_in, conv_dw_w, conv_dw_b,
           conv_ln_g, conv_ln_b, conv_w_proj, attn_sinks, attn_w_o, gate_b, w_out, ffn2_norm, ffn2_w_gate,
           ffn2_w_up, ffn2_w_down, final_norm):
    batch, seq, d = x.shape
    assert d == D_MODEL and seq % MIX_ROWS == 0 and (batch * seq) % FFN_ROWS == 0 and MIX_ROWS % WINDOW == 0
    depth = ffn1_norm.shape[0]
    t = batch * seq
    xt = x.reshape(t, D_MODEL)
    pos = positions.reshape(t // MIX_ROWS, 1, MIX_ROWS).astype(jnp.int32)
    inv_freq = ROPE_THETA ** (-jnp.arange(HALF, dtype=F32) / HALF)
    freq = jnp.broadcast_to(inv_freq[:, None], (HALF, MIX_ROWS))
    row = lambda a: a.reshape(1, -1)
    bf = lambda a: a.astype(BF16)
    fg = row(final_norm)
    for l in range(depth):
        xt = _ffn(xt, row(ffn1_norm[l]), bf(ffn1_w_gate[l]), bf(ffn1_w_up[l]), bf(ffn1_w_down[l]), fg,
                  final_norm=False)
        sinks = jnp.repeat(attn_sinks[l].reshape(N_KV_HEADS, GROUP), WINDOW, axis=1).reshape(
            N_KV_HEADS, 1, GROUP * WINDOW)
        xt = _mixer(xt, pos, freq, row(mix_norm[l]), bf(w_in[l]), conv_dw_w[l],
                    row(conv_dw_b[l]), row(conv_ln_g[l]), row(conv_ln_b[l]), bf(conv_w_proj[l]), sinks,
                    bf(attn_w_o[l]), row(gate_b[l]), bf(w_out[l]), batch=batch, seq=seq)
        xt = _ffn(xt, row(ffn2_norm[l]), bf(ffn2_w_gate[l]), bf(ffn2_w_up[l]), bf(ffn2_w_down[l]), fg,
                  final_norm=(l == depth - 1))
    return xt.reshape(batch, seq, D_MODEL)
```

```python
import functools

import jax
import jax.numpy as jnp
from jax import lax
from jax.experimental import pallas as pl
from jax.experimental.pallas import tpu as pltpu

D_MODEL = 1024
D_FF = 2816
D_CONV = D_MODEL
CONV_WIDTH = 31
HEAD_DIM = 64
HALF = HEAD_DIM // 2
N_HEADS = D_MODEL // HEAD_DIM
N_KV_HEADS = 4
GROUP = N_HEADS // N_KV_HEADS
D_Q = N_HEADS * HEAD_DIM
D_KV = N_KV_HEADS * HEAD_DIM
WINDOW = 128
ROPE_THETA = 10000.0
EPS = 1e-6
LN_EPS = 1e-5
NEG_INF = -1e30
LOG2E = 1.4426950408889634

_SPLITS = (D_CONV, D_CONV, D_Q, D_KV, D_KV, D_MODEL, D_MODEL)
_OFF = tuple(sum(_SPLITS[:i]) for i in range(len(_SPLITS) + 1))
D_IN = _OFF[-1]
D_QKV = D_Q + 2 * D_KV

LANES = 128
SUBLANES = 8
MXU_TILE = 256
V7X_VMEM_BYTES = 64 * 1024 * 1024

FFN_ROWS = 1024
MIX_ROWS = 512
PART_ROWS = 256
CONV_HALO = 32
CONV_CHUNK = 64
ROW_STRIDE = 4
FF_CHUNK = 4 * MXU_TILE
FFN_STAGE_ELEMS = D_MODEL * D_FF // 8
N_SLABS = D_CONV // LANES

F32 = jnp.float32
BF16 = jnp.bfloat16

_NT = (((1,), (1,)), ((), ()))
_TN = (((0,), (0,)), ((), ()))


def _rms(x, g):
    return x * lax.rsqrt(jnp.mean(x * x, axis=-1, keepdims=True) + EPS) * g


def _sigmoid(x):
    return 1.0 / (1.0 + jnp.exp2(x * -LOG2E))


def _dot(a, b, dims=None):
    if dims is None:
        return jnp.dot(a, b, preferred_element_type=F32)
    return lax.dot_general(a, b, dims, preferred_element_type=F32)


def _fetch_as_bf16(src_hbm, dst_ref, stage_ref, sem_ref):
    chunk_rows = stage_ref.shape[1]
    n_chunks = src_hbm.shape[0] // chunk_rows

    def copy(i):
        return pltpu.make_async_copy(src_hbm.at[pl.ds(i * chunk_rows, chunk_rows), :], stage_ref.at[i % 2],
                                     sem_ref.at[i % 2])

    copy(0).start()
    for i in range(n_chunks):
        if i + 1 < n_chunks:
            copy(i + 1).start()
        copy(i).wait()
        dst_ref[i * chunk_rows:(i + 1) * chunk_rows, :] = stage_ref[i % 2].astype(BF16)


def _ffn_body(x_ref, g_ref, wg_hbm, wu_hbm, wd_hbm, fg_ref, o_ref, wg_ref, wu_ref, wd_ref, stage_in_ref,
              stage_out_ref, sem_ref, *, final_norm):
    @pl.when(pl.program_id(0) == 0)
    def _():
        _fetch_as_bf16(wg_hbm, wg_ref, stage_in_ref, sem_ref.at[0])
        _fetch_as_bf16(wu_hbm, wu_ref, stage_in_ref, sem_ref.at[0])
        _fetch_as_bf16(wd_hbm, wd_ref, stage_out_ref, sem_ref.at[1])

    x = x_ref[...]
    h = _rms(x, g_ref[...]).astype(BF16)
    acc = None
    for c0 in range(0, D_FF, FF_CHUNK):
        c1 = min(c0 + FF_CHUNK, D_FF)
        gate = _dot(h, wg_ref[:, c0:c1])
        up = _dot(h, wu_ref[:, c0:c1])
        a = (gate * _sigmoid(gate) * up).astype(BF16)
        y = _dot(a, wd_ref[c0:c1, :])
        acc = y if acc is None else acc + y
    out = x + 0.5 * acc
    if final_norm:
        out = _rms(out, fg_ref[...])
    o_ref[...] = out


def _resident(shape):
    return pl.BlockSpec(shape, lambda *_: (0,) * len(shape), pipeline_mode=pl.Buffered(1))


def _ffn(x, norm_g, w_gate, w_up, w_down, final_g, *, final_norm):
    t = x.shape[0]
    rows = pl.BlockSpec((FFN_ROWS, D_MODEL), lambda i: (i, 0))
    weight_bytes = 3 * D_MODEL * D_FF * 2
    stage_bytes = 2 * 2 * FFN_STAGE_ELEMS * 4
    tile_bytes = FFN_ROWS * D_MODEL * 4
    temp_bytes = FFN_ROWS * FF_CHUNK * 4 * 4 + 4 * tile_bytes
    vmem = weight_bytes + stage_bytes + 4 * tile_bytes + temp_bytes
    in_hbm = pl.BlockSpec(memory_space=pl.ANY)
    return pl.pallas_call(
        functools.partial(_ffn_body, final_norm=final_norm),
        grid=(t // FFN_ROWS,),
        in_specs=[rows, _resident((1, D_MODEL)), in_hbm, in_hbm, in_hbm, _resident((1, D_MODEL))],
        out_specs=rows,
        out_shape=jax.ShapeDtypeStruct((t, D_MODEL), F32),
        scratch_shapes=[
            pltpu.VMEM((D_MODEL, D_FF), BF16),
            pltpu.VMEM((D_MODEL, D_FF), BF16),
            pltpu.VMEM((D_FF, D_MODEL), BF16),
            pltpu.VMEM((2, FFN_STAGE_ELEMS // D_FF, D_FF), F32),
            pltpu.VMEM((2, FFN_STAGE_ELEMS // D_MODEL, D_MODEL), F32),
            pltpu.SemaphoreType.DMA((2, 2)),
        ],
        compiler_params=pltpu.CompilerParams(dimension_semantics=("arbitrary",),
                                             vmem_limit_bytes=min(vmem, V7X_VMEM_BYTES)),
        name="ffn_final" if final_norm else "ffn",
    )(x, norm_g, w_gate, w_up, w_down, final_g)


def _causal_conv_chunk(c0, u_ref, dww_ref, dwb_ref, conv_ref):
    per_reg = SUBLANES * ROW_STRIDE
    starts = [c0 + b0 + j for b0 in range(0, CONV_CHUNK, per_reg) for j in range(ROW_STRIDE)]
    for slab in range(N_SLABS):
        l0 = slab * LANES
        accs = [jnp.broadcast_to(dwb_ref[:, l0:l0 + LANES], (SUBLANES, LANES)) for _ in starts]
        for k in range(CONV_WIDTH):
            w_k = dww_ref[k:k + 1, l0:l0 + LANES]
            lag = CONV_WIDTH - 1 - k
            for i, t0 in enumerate(starts):
                tap = u_ref[slab, pl.ds(CONV_HALO + t0 - lag, SUBLANES, stride=ROW_STRIDE), :]
                accs[i] = accs[i] + w_k * tap
        for acc, t0 in zip(accs, starts):
            conv_ref[slab, pl.ds(t0, SUBLANES, stride=ROW_STRIDE), :] = acc


def _mixer_body(x_ref, pos_ref, freq_ref, ng_ref, win_ref, dww_ref, dwb_ref, lng_ref, lnb_ref,
                wproj_ref, sink_ref, wo_ref, gb_ref, wout_ref, o_ref,
                u_ref, conv_ref, qt_ref, kt_ref, vt_ref, at_ref):
    first_tile = pl.program_id(1) == 0

    @pl.when(first_tile)
    def _():
        u_ref[:, 0:CONV_HALO, :] = jnp.zeros((N_SLABS, CONV_HALO, LANES), F32)
        kt_ref[:, 0:WINDOW] = jnp.zeros((D_KV, WINDOW), BF16)
        vt_ref[:, 0:WINDOW] = jnp.zeros((D_KV, WINDOW), BF16)

    @pl.when(jnp.logical_not(first_tile))
    def _():
        u_ref[:, 0:CONV_HALO, :] = u_ref[:, MIX_ROWS:MIX_ROWS + CONV_HALO, :]
        kt_ref[:, 0:WINDOW] = kt_ref[:, MIX_ROWS:MIX_ROWS + WINDOW]
        vt_ref[:, 0:WINDOW] = vt_ref[:, MIX_ROWS:MIX_ROWS + WINDOW]

    x = x_ref[...]
    h = _rms(x, ng_ref[...]).astype(BF16)

    def proj(i):
        return _dot(h, win_ref[:, _OFF[i]:_OFF[i + 1]])

    u = proj(0) * _sigmoid(proj(1))
    for slab in range(N_SLABS):
        u_ref[slab, CONV_HALO:, :] = u[:, slab * LANES:(slab + 1) * LANES]

    qkv_t = _dot(h, win_ref[:, _OFF[2]:_OFF[5]]).T
    ang = freq_ref[...] * pos_ref[...].astype(F32)
    cos = jnp.cos(ang)
    sin = jnp.sin(ang)

    def rope_head(r0, c, s):
        x1 = qkv_t[r0:r0 + HALF]
        x2 = qkv_t[r0 + HALF:r0 + HEAD_DIM]
        return (x1 * c - x2 * s).astype(BF16), (x2 * c + x1 * s).astype(BF16)

    q_scale = HEAD_DIM ** -0.5 * LOG2E
    cos_q, sin_q = cos * q_scale, sin * q_scale
    for hd in range(N_HEADS):
        r0 = hd * HEAD_DIM
        qt_ref[r0:r0 + HALF, :], qt_ref[r0 + HALF:r0 + HEAD_DIM, :] = rope_head(r0, cos_q, sin_q)
    for kh in range(N_KV_HEADS):
        r0 = kh * HEAD_DIM
        kt_ref[r0:r0 + HALF, WINDOW:], kt_ref[r0 + HALF:r0 + HEAD_DIM, WINDOW:] = rope_head(D_Q + r0, cos, sin)
    vt_ref[:, WINDOW:] = qkv_t[D_Q + D_KV:D_QKV].astype(BF16)

    gate_c = _sigmoid(proj(5) + gb_ref[:, 0:D_MODEL])
    gate_a = _sigmoid(proj(6) + gb_ref[:, D_MODEL:2 * D_MODEL])

    kc = lax.broadcasted_iota(jnp.int32, (2 * WINDOW, GROUP * WINDOW), 0)
    qq = lax.broadcasted_iota(jnp.int32, (2 * WINDOW, GROUP * WINDOW), 1) % WINDOW
    band = (kc > qq) & (kc <= qq + WINDOW)
    band_first = band & ((kc >= WINDOW) | jnp.logical_not(first_tile))
    bias = jnp.where(band, 0.0, NEG_INF)
    bias_first = jnp.where(band_first, 0.0, NEG_INF)

    def attend(n, kh):
        mask_bias = bias_first if n == 0 else bias
        t0 = n * WINDOW
        f0 = kh * HEAD_DIM
        h0 = kh * GROUP * HEAD_DIM
        k_band = kt_ref[f0:f0 + HEAD_DIM, t0:t0 + 2 * WINDOW]
        v_band = vt_ref[f0:f0 + HEAD_DIM, t0:t0 + 2 * WINDOW]
        q_grp = jnp.concatenate(
            [qt_ref[h0 + g * HEAD_DIM:h0 + (g + 1) * HEAD_DIM, t0:t0 + WINDOW] for g in range(GROUP)],
            axis=1)
        s = _dot(k_band, q_grp, _TN) + mask_bias
        sink = sink_ref[kh] * LOG2E
        m = jnp.maximum(jnp.max(s, axis=0, keepdims=True), sink)
        p = jnp.exp2(s - m)
        inv = 1.0 / (jnp.sum(p, axis=0, keepdims=True) + jnp.exp2(sink - m))
        o_t = _dot(v_band, p.astype(BF16)) * inv
        for g in range(GROUP):
            at_ref[h0 + g * HEAD_DIM:h0 + (g + 1) * HEAD_DIM, t0:t0 + WINDOW] = (
                o_t[:, g * WINDOW:(g + 1) * WINDOW].astype(BF16))

    for r0 in range(0, MIX_ROWS, PART_ROWS):
        rows = slice(r0, r0 + PART_ROWS)
        for c0 in range(r0, r0 + PART_ROWS, CONV_CHUNK):
            _causal_conv_chunk(c0, u_ref, dww_ref, dwb_ref, conv_ref)
        for n in range(r0 // WINDOW, (r0 + PART_ROWS) // WINDOW):
            for kh in range(N_KV_HEADS):
                attend(n, kh)

        cv = jnp.concatenate([conv_ref[slab, rows, :] for slab in range(N_SLABS)], axis=1)
        mu = jnp.mean(cv, axis=-1, keepdims=True)
        cen = cv - mu
        var = jnp.mean(cen * cen, axis=-1, keepdims=True)
        ln = cen * lax.rsqrt(var + LN_EPS) * lng_ref[...] + lnb_ref[...]
        conv_out = _dot((ln * _sigmoid(ln)).astype(BF16), wproj_ref[...])

        attn_out = _dot(at_ref[:, rows], wo_ref[...], _TN)

        merged = (gate_c[rows] * conv_out + gate_a[rows] * attn_out).astype(BF16)
        o_ref[rows, :] = x[rows] + _dot(merged, wout_ref[...])


def _mixer(x, pos, freq, norm_g, w_in, dw_w, dw_b, ln_g, ln_b, w_proj, sinks, w_o, gate_b, w_out, *,
           batch, seq):
    tiles = seq // MIX_ROWS
    rows = pl.BlockSpec((MIX_ROWS, D_MODEL), lambda b, j: (b * tiles + j, 0))
    pos_spec = pl.BlockSpec((None, 1, MIX_ROWS), lambda b, j: (b * tiles + j, 0, 0))
    weight_bytes = (D_MODEL * D_IN + 3 * D_MODEL * D_MODEL) * 2
    tile_bytes = MIX_ROWS * D_MODEL * 4
    scratch_bytes = ((MIX_ROWS + CONV_HALO) * D_CONV * 4 + tile_bytes
                     + 2 * D_Q * MIX_ROWS * 2 + 2 * D_KV * (MIX_ROWS + WINDOW) * 2)
    temp_bytes = 12 * tile_bytes
    vmem = weight_bytes + 4 * tile_bytes + scratch_bytes + temp_bytes
    return pl.pallas_call(
        _mixer_body,
        grid=(batch, tiles),
        in_specs=[rows, pos_spec, _resident((HALF, MIX_ROWS)), _resident((1, D_MODEL)),
                  _resident((D_MODEL, D_IN)),
                  _resident((CONV_WIDTH, D_CONV)), _resident((1, D_CONV)), _resident((1, D_CONV)),
                  _resident((1, D_CONV)), _resident((D_CONV, D_MODEL)),
                  _resident((N_KV_HEADS, 1, GROUP * WINDOW)),
                  _resident((D_Q, D_MODEL)), _resident((1, 2 * D_MODEL)), _resident((D_MODEL, D_MODEL))],
        out_specs=rows,
        out_shape=jax.ShapeDtypeStruct((batch * seq, D_MODEL), F32),
        scratch_shapes=[
            pltpu.VMEM((N_SLABS, CONV_HALO + MIX_ROWS, LANES), F32),
            pltpu.VMEM((N_SLABS, MIX_ROWS, LANES), F32),
            pltpu.VMEM((D_Q, MIX_ROWS), BF16),
            pltpu.VMEM((D_KV, WINDOW + MIX_ROWS), BF16),
            pltpu.VMEM((D_KV, WINDOW + MIX_ROWS), BF16),
            pltpu.VMEM((D_Q, MIX_ROWS), BF16),
        ],
        compiler_params=pltpu.CompilerParams(dimension_semantics=("arbitrary", "arbitrary"),
                                             vmem_limit_bytes=min(vmem, V7X_VMEM_BYTES)),
        name="mixer",
    )(x, pos, freq, norm_g, w_in, dw_w, dw_b, ln_g, ln_b, w_proj, sinks, w_o, gate_b, w_out)


def kernel(x, positions, ffn1_norm, ffn1_w_gate, ffn1_w_up, ffn1_w_down, mix_norm, w_in, conv_dw_w, conv_dw_b,
           conv_ln_g, conv_ln_b, conv_w_proj, attn_sinks, attn_w_o, gate_b, w_out, ffn2_norm, ffn2_w_gate,
           ffn2_w_up, ffn2_w_down, final_norm):
    batch, seq, d = x.shape
    assert d == D_MODEL and seq % MIX_ROWS == 0 and (batch * seq) % FFN_ROWS == 0 and MIX_ROWS % WINDOW == 0
    depth = ffn1_norm.shape[0]
    t = batch * seq
    xt = x.reshape(t, D_MODEL)
    pos = positions.reshape(t // MIX_ROWS, 1, MIX_ROWS).astype(jnp.int32)
    inv_freq = ROPE_THETA ** (-jnp.arange(HALF, dtype=F32) / HALF)
    freq = jnp.broadcast_to(inv_freq[:, None], (HALF, MIX_ROWS))
    row = lambda a: a.reshape(1, -1)
    bf = lambda a: a.astype(BF16)
    fg = row(final_norm)
    for l in range(depth):
        xt = _ffn(xt, row(ffn1_norm[l]), ffn1_w_gate[l], ffn1_w_up[l], ffn1_w_down[l], fg,
                  final_norm=False)
        sinks = jnp.repeat(attn_sinks[l].reshape(N_KV_HEADS, GROUP), WINDOW, axis=1).reshape(
            N_KV_HEADS, 1, GROUP * WINDOW)
        xt = _mixer(xt, pos, freq, row(mix_norm[l]), bf(w_in[l]), conv_dw_w[l],
                    row(conv_dw_b[l]), row(conv_ln_g[l]), row(conv_ln_b[l]), bf(conv_w_proj[l]), sinks,
                    bf(attn_w_o[l]), row(gate_b[l]), bf(w_out[l]), batch=batch, seq=seq)
        xt = _ffn(xt, row(ffn2_norm[l]), ffn2_w_gate[l], ffn2_w_up[l], ffn2_w_down[l], fg,
                  final_norm=(l == depth - 1))
    return xt.reshape(batch, seq, D_MODEL)
```

```python
import functools

import jax
import jax.numpy as jnp
from jax import lax
from jax.experimental import pallas as pl
from jax.experimental.pallas import tpu as pltpu

D_MODEL = 1024
D_FF = 2816
D_CONV = D_MODEL
CONV_WIDTH = 31
HEAD_DIM = 64
HALF = HEAD_DIM // 2
N_HEADS = D_MODEL // HEAD_DIM
N_KV_HEADS = 4
GROUP = N_HEADS // N_KV_HEADS
D_Q = N_HEADS * HEAD_DIM
D_KV = N_KV_HEADS * HEAD_DIM
WINDOW = 128
ROPE_THETA = 10000.0
EPS = 1e-6
LN_EPS = 1e-5
NEG_INF = -1e30
LOG2E = 1.4426950408889634

_SPLITS = (D_CONV, D_CONV, D_Q, D_KV, D_KV, D_MODEL, D_MODEL)
_OFF = tuple(sum(_SPLITS[:i]) for i in range(len(_SPLITS) + 1))
D_IN = _OFF[-1]
D_QKV = D_Q + 2 * D_KV

LANES = 128
SUBLANES = 8
MXU_TILE = 256
V7X_VMEM_BYTES = 64 * 1024 * 1024

FFN_ROWS = 1024
MIX_ROWS = 512
PART_ROWS = 256
CONV_HALO = 32
CONV_CHUNK = 64
ROW_STRIDE = 4
FF_CHUNK = 4 * MXU_TILE
N_SLABS = D_CONV // LANES
STAGE_ELEMS = D_MODEL * D_FF // 8
STAGE_SLOTS = 4
SQUARE_STAGE_ROWS = 256

F32 = jnp.float32
BF16 = jnp.bfloat16

_TN = (((0,), (0,)), ((), ()))


def _rms(x, g):
    return x * lax.rsqrt(jnp.mean(x * x, axis=-1, keepdims=True) + EPS) * g


def _sigmoid(x):
    return 1.0 / (1.0 + jnp.exp2(x * -LOG2E))


def _dot(a, b, dims=None):
    if dims is None:
        return jnp.dot(a, b, preferred_element_type=F32)
    return lax.dot_general(a, b, dims, preferred_element_type=F32)


def _resident(shape):
    return pl.BlockSpec(shape, lambda *_: (0,) * len(shape), pipeline_mode=pl.Buffered(1))


def _fetch_as_bf16(src_hbm, dst_ref, stage_ref, sem_ref):
    n_slots, chunk_rows, _ = stage_ref.shape
    n_chunks = src_hbm.shape[0] // chunk_rows

    def copy(i):
        slot = i % n_slots
        return pltpu.make_async_copy(src_hbm.at[pl.ds(i * chunk_rows, chunk_rows), :], stage_ref.at[slot],
                                     sem_ref.at[slot])

    for i in range(min(n_slots - 1, n_chunks)):
        copy(i).start()
    for i in range(n_chunks):
        if i + n_slots - 1 < n_chunks:
            copy(i + n_slots - 1).start()
        copy(i).wait()
        dst_ref[i * chunk_rows:(i + 1) * chunk_rows, :] = stage_ref[i % n_slots].astype(BF16)


def _ffn_body(x_ref, g_ref, wg_hbm, wu_hbm, wd_hbm, fg_ref, o_ref, wg_ref, wu_ref, wd_ref, stage_in_ref,
              stage_out_ref, sem_ref, *, final_norm):
    @pl.when(pl.program_id(0) == 0)
    def _():
        _fetch_as_bf16(wg_hbm, wg_ref, stage_in_ref, sem_ref.at[0])
        _fetch_as_bf16(wu_hbm, wu_ref, stage_in_ref, sem_ref.at[0])
        _fetch_as_bf16(wd_hbm, wd_ref, stage_out_ref, sem_ref.at[1])

    x = x_ref[...]
    h = _rms(x, g_ref[...]).astype(BF16)
    acc = None
    for c0 in range(0, D_FF, FF_CHUNK):
        c1 = min(c0 + FF_CHUNK, D_FF)
        gate = _dot(h, wg_ref[:, c0:c1])
        up = _dot(h, wu_ref[:, c0:c1])
        a = (gate * _sigmoid(gate) * up).astype(BF16)
        y = _dot(a, wd_ref[c0:c1, :])
        acc = y if acc is None else acc + y
    out = x + 0.5 * acc
    if final_norm:
        out = _rms(out, fg_ref[...])
    o_ref[...] = out


def _ffn(x, norm_g, w_gate, w_up, w_down, final_g, *, final_norm):
    t = x.shape[0]
    rows = pl.BlockSpec((FFN_ROWS, D_MODEL), lambda i: (i, 0))
    weight_bytes = 3 * D_MODEL * D_FF * 2
    stage_bytes = 2 * STAGE_SLOTS * STAGE_ELEMS * 4
    tile_bytes = FFN_ROWS * D_MODEL * 4
    temp_bytes = FFN_ROWS * FF_CHUNK * 4 * 4 + 4 * tile_bytes
    vmem = weight_bytes + stage_bytes + 4 * tile_bytes + temp_bytes
    in_hbm = pl.BlockSpec(memory_space=pl.ANY)
    return pl.pallas_call(
        functools.partial(_ffn_body, final_norm=final_norm),
        grid=(t // FFN_ROWS,),
        in_specs=[rows, _resident((1, D_MODEL)), in_hbm, in_hbm, in_hbm, _resident((1, D_MODEL))],
        out_specs=rows,
        out_shape=jax.ShapeDtypeStruct((t, D_MODEL), F32),
        scratch_shapes=[
            pltpu.VMEM((D_MODEL, D_FF), BF16),
            pltpu.VMEM((D_MODEL, D_FF), BF16),
            pltpu.VMEM((D_FF, D_MODEL), BF16),
            pltpu.VMEM((STAGE_SLOTS, STAGE_ELEMS // D_FF, D_FF), F32),
            pltpu.VMEM((STAGE_SLOTS, STAGE_ELEMS // D_MODEL, D_MODEL), F32),
            pltpu.SemaphoreType.DMA((2, STAGE_SLOTS)),
        ],
        compiler_params=pltpu.CompilerParams(dimension_semantics=("arbitrary",),
                                             vmem_limit_bytes=min(vmem, V7X_VMEM_BYTES)),
        name="ffn_final" if final_norm else "ffn",
    )(x, norm_g, w_gate, w_up, w_down, final_g)


def _causal_conv_chunk(c0, u_ref, dww_ref, dwb_ref, conv_ref):
    per_reg = SUBLANES * ROW_STRIDE
    starts = [c0 + b0 + j for b0 in range(0, CONV_CHUNK, per_reg) for j in range(ROW_STRIDE)]
    for slab in range(N_SLABS):
        l0 = slab * LANES
        accs = [jnp.broadcast_to(dwb_ref[:, l0:l0 + LANES], (SUBLANES, LANES)) for _ in starts]
        for k in range(CONV_WIDTH):
            w_k = dww_ref[k:k + 1, l0:l0 + LANES]
            lag = CONV_WIDTH - 1 - k
            for i, t0 in enumerate(starts):
                tap = u_ref[slab, pl.ds(CONV_HALO + t0 - lag, SUBLANES, stride=ROW_STRIDE), :]
                accs[i] = accs[i] + w_k * tap
        for acc, t0 in zip(accs, starts):
            conv_ref[slab, pl.ds(t0, SUBLANES, stride=ROW_STRIDE), :] = acc


def _mixer_body(x_ref, pos_ref, freq_ref, ng_ref, win_hbm, dww_ref, dwb_ref, lng_ref, lnb_ref,
                wproj_hbm, sink_ref, wo_hbm, gb_ref, wout_hbm, o_ref,
                u_ref, conv_ref, qt_ref, kt_ref, vt_ref, at_ref,
                win_ref, wproj_ref, wo_ref, wout_ref, stage_in_ref, stage_sq_ref, sem_ref):
    first_tile = pl.program_id(1) == 0

    @pl.when(first_tile & (pl.program_id(0) == 0))
    def _():
        _fetch_as_bf16(win_hbm, win_ref, stage_in_ref, sem_ref.at[0])
        for src, dst in ((wproj_hbm, wproj_ref), (wo_hbm, wo_ref), (wout_hbm, wout_ref)):
            _fetch_as_bf16(src, dst, stage_sq_ref, sem_ref.at[1])

    @pl.when(first_tile)
    def _():
        u_ref[:, 0:CONV_HALO, :] = jnp.zeros((N_SLABS, CONV_HALO, LANES), F32)
        kt_ref[:, 0:WINDOW] = jnp.zeros((D_KV, WINDOW), BF16)
        vt_ref[:, 0:WINDOW] = jnp.zeros((D_KV, WINDOW), BF16)

    @pl.when(jnp.logical_not(first_tile))
    def _():
        u_ref[:, 0:CONV_HALO, :] = u_ref[:, MIX_ROWS:MIX_ROWS + CONV_HALO, :]
        kt_ref[:, 0:WINDOW] = kt_ref[:, MIX_ROWS:MIX_ROWS + WINDOW]
        vt_ref[:, 0:WINDOW] = vt_ref[:, MIX_ROWS:MIX_ROWS + WINDOW]

    x = x_ref[...]
    h = _rms(x, ng_ref[...]).astype(BF16)

    def proj(i):
        return _dot(h, win_ref[:, _OFF[i]:_OFF[i + 1]])

    u = proj(0) * _sigmoid(proj(1))
    for slab in range(N_SLABS):
        u_ref[slab, CONV_HALO:, :] = u[:, slab * LANES:(slab + 1) * LANES]

    qkv_t = _dot(h, win_ref[:, _OFF[2]:_OFF[5]]).T
    ang = freq_ref[...] * pos_ref[...].astype(F32)
    cos = jnp.cos(ang)
    sin = jnp.sin(ang)

    def rope_head(r0, c, s):
        x1 = qkv_t[r0:r0 + HALF]
        x2 = qkv_t[r0 + HALF:r0 + HEAD_DIM]
        return (x1 * c - x2 * s).astype(BF16), (x2 * c + x1 * s).astype(BF16)

    q_scale = HEAD_DIM ** -0.5 * LOG2E
    cos_q, sin_q = cos * q_scale, sin * q_scale
    for hd in range(N_HEADS):
        r0 = hd * HEAD_DIM
        qt_ref[r0:r0 + HALF, :], qt_ref[r0 + HALF:r0 + HEAD_DIM, :] = rope_head(r0, cos_q, sin_q)
    for kh in range(N_KV_HEADS):
        r0 = kh * HEAD_DIM
        kt_ref[r0:r0 + HALF, WINDOW:], kt_ref[r0 + HALF:r0 + HEAD_DIM, WINDOW:] = rope_head(D_Q + r0, cos, sin)
    vt_ref[:, WINDOW:] = qkv_t[D_Q + D_KV:D_QKV].astype(BF16)

    gate_c = _sigmoid(proj(5) + gb_ref[:, 0:D_MODEL])
    gate_a = _sigmoid(proj(6) + gb_ref[:, D_MODEL:2 * D_MODEL])

    kc = lax.broadcasted_iota(jnp.int32, (2 * WINDOW, GROUP * WINDOW), 0)
    qq = lax.broadcasted_iota(jnp.int32, (2 * WINDOW, GROUP * WINDOW), 1) % WINDOW
    band = (kc > qq) & (kc <= qq + WINDOW)
    band_first = band & ((kc >= WINDOW) | jnp.logical_not(first_tile))
    bias = jnp.where(band, 0.0, NEG_INF)
    bias_first = jnp.where(band_first, 0.0, NEG_INF)

    def attend(n, kh):
        mask_bias = bias_first if n == 0 else bias
        t0 = n * WINDOW
        f0 = kh * HEAD_DIM
        h0 = kh * GROUP * HEAD_DIM
        k_band = kt_ref[f0:f0 + HEAD_DIM, t0:t0 + 2 * WINDOW]
        v_band = vt_ref[f0:f0 + HEAD_DIM, t0:t0 + 2 * WINDOW]
        q_grp = jnp.concatenate(
            [qt_ref[h0 + g * HEAD_DIM:h0 + (g + 1) * HEAD_DIM, t0:t0 + WINDOW] for g in range(GROUP)],
            axis=1)
        s = _dot(k_band, q_grp, _TN) + mask_bias
        sink = sink_ref[kh] * LOG2E
        m = jnp.maximum(jnp.max(s, axis=0, keepdims=True), sink)
        p = jnp.exp2(s - m)
        inv = 1.0 / (jnp.sum(p, axis=0, keepdims=True) + jnp.exp2(sink - m))
        o_t = _dot(v_band, p.astype(BF16)) * inv
        for g in range(GROUP):
            at_ref[h0 + g * HEAD_DIM:h0 + (g + 1) * HEAD_DIM, t0:t0 + WINDOW] = (
                o_t[:, g * WINDOW:(g + 1) * WINDOW].astype(BF16))

    for r0 in range(0, MIX_ROWS, PART_ROWS):
        rows = slice(r0, r0 + PART_ROWS)
        for c0 in range(r0, r0 + PART_ROWS, CONV_CHUNK):
            _causal_conv_chunk(c0, u_ref, dww_ref, dwb_ref, conv_ref)
        for n in range(r0 // WINDOW, (r0 + PART_ROWS) // WINDOW):
            for kh in range(N_KV_HEADS):
                attend(n, kh)

        cv = jnp.concatenate([conv_ref[slab, rows, :] for slab in range(N_SLABS)], axis=1)
        mu = jnp.mean(cv, axis=-1, keepdims=True)
        cen = cv - mu
        var = jnp.mean(cen * cen, axis=-1, keepdims=True)
        ln = cen * lax.rsqrt(var + LN_EPS) * lng_ref[...] + lnb_ref[...]
        conv_out = _dot((ln * _sigmoid(ln)).astype(BF16), wproj_ref[...])

        attn_out = _dot(at_ref[:, rows], wo_ref[...], _TN)

        merged = (gate_c[rows] * conv_out + gate_a[rows] * attn_out).astype(BF16)
        o_ref[rows, :] = x[rows] + _dot(merged, wout_ref[...])


def _mixer(x, pos, freq, norm_g, w_in, dw_w, dw_b, ln_g, ln_b, w_proj, sinks, w_o, gate_b, w_out, *,
           batch, seq):
    tiles = seq // MIX_ROWS
    rows = pl.BlockSpec((MIX_ROWS, D_MODEL), lambda b, j: (b * tiles + j, 0))
    pos_spec = pl.BlockSpec((None, 1, MIX_ROWS), lambda b, j: (b * tiles + j, 0, 0))
    weight_bytes = (D_MODEL * D_IN + 3 * D_MODEL * D_MODEL) * 2
    stage_bytes = STAGE_SLOTS * (STAGE_ELEMS + SQUARE_STAGE_ROWS * D_MODEL) * 4
    tile_bytes = MIX_ROWS * D_MODEL * 4
    scratch_bytes = ((MIX_ROWS + CONV_HALO) * D_CONV * 4 + tile_bytes
                     + 2 * D_Q * MIX_ROWS * 2 + 2 * D_KV * (MIX_ROWS + WINDOW) * 2)
    temp_bytes = 12 * tile_bytes
    vmem = weight_bytes + stage_bytes + 4 * tile_bytes + scratch_bytes + temp_bytes
    in_hbm = pl.BlockSpec(memory_space=pl.ANY)
    return pl.pallas_call(
        _mixer_body,
        grid=(batch, tiles),
        in_specs=[rows, pos_spec, _resident((HALF, MIX_ROWS)), _resident((1, D_MODEL)),
                  in_hbm,
                  _resident((CONV_WIDTH, D_CONV)), _resident((1, D_CONV)), _resident((1, D_CONV)),
                  _resident((1, D_CONV)), in_hbm,
                  _resident((N_KV_HEADS, 1, GROUP * WINDOW)),
                  in_hbm, _resident((1, 2 * D_MODEL)), in_hbm],
        out_specs=rows,
        out_shape=jax.ShapeDtypeStruct((batch * seq, D_MODEL), F32),
        scratch_shapes=[
            pltpu.VMEM((N_SLABS, CONV_HALO + MIX_ROWS, LANES), F32),
            pltpu.VMEM((N_SLABS, MIX_ROWS, LANES), F32),
            pltpu.VMEM((D_Q, MIX_ROWS), BF16),
            pltpu.VMEM((D_KV, WINDOW + MIX_ROWS), BF16),
            pltpu.VMEM((D_KV, WINDOW + MIX_ROWS), BF16),
            pltpu.VMEM((D_Q, MIX_ROWS), BF16),
            pltpu.VMEM((D_MODEL, D_IN), BF16),
            pltpu.VMEM((D_CONV, D_MODEL), BF16),
            pltpu.VMEM((D_Q, D_MODEL), BF16),
            pltpu.VMEM((D_MODEL, D_MODEL), BF16),
            pltpu.VMEM((STAGE_SLOTS, STAGE_ELEMS // D_IN, D_IN), F32),
            pltpu.VMEM((STAGE_SLOTS, SQUARE_STAGE_ROWS, D_MODEL), F32),
            pltpu.SemaphoreType.DMA((2, STAGE_SLOTS)),
        ],
        compiler_params=pltpu.CompilerParams(dimension_semantics=("arbitrary", "arbitrary"),
                                             vmem_limit_bytes=min(vmem, V7X_VMEM_BYTES)),
        name="mixer",
    )(x, pos, freq, norm_g, w_in, dw_w, dw_b, ln_g, ln_b, w_proj, sinks, w_o, gate_b, w_out)


def kernel(x, positions, ffn1_norm, ffn1_w_gate, ffn1_w_up, ffn1_w_down, mix_norm, w_in, conv_dw_w, conv_dw_b,
           conv_ln_g, conv_ln_b, conv_w_proj, attn_sinks, attn_w_o, gate_b, w_out, ffn2_norm, ffn2_w_gate,
           ffn2_w_up, ffn2_w_down, final_norm):
    batch, seq, d = x.shape
    assert d == D_MODEL and seq % MIX_ROWS == 0 and (batch * seq) % FFN_ROWS == 0 and MIX_ROWS % WINDOW == 0
    depth = ffn1_norm.shape[0]
    t = batch * seq
    xt = x.reshape(t, D_MODEL)
    pos = positions.reshape(t // MIX_ROWS, 1, MIX_ROWS).astype(jnp.int32)
    inv_freq = ROPE_THETA ** (-jnp.arange(HALF, dtype=F32) / HALF)
    freq = jnp.broadcast_to(inv_freq[:, None], (HALF, MIX_ROWS))
    row = lambda a: a.reshape(1, -1)
    fg = row(final_norm)
    for l in range(depth):
        xt = _ffn(xt, row(ffn1_norm[l]), ffn1_w_gate[l], ffn1_w_up[l], ffn1_w_down[l], fg,
                  final_norm=False)
        sinks = jnp.repeat(attn_sinks[l].reshape(N_KV_HEADS, GROUP), WINDOW, axis=1).reshape(
            N_KV_HEADS, 1, GROUP * WINDOW)
        xt = _mixer(xt, pos, freq, row(mix_norm[l]), w_in[l], conv_dw_w[l],
                    row(conv_dw_b[l]), row(conv_ln_g[l]), row(conv_ln_b[l]), conv_w_proj[l], sinks,
                    attn_w_o[l], row(gate_b[l]), w_out[l], batch=batch, seq=seq)
        xt = _ffn(xt, row(ffn2_norm[l]), ffn2_w_gate[l], ffn2_w_up[l], ffn2_w_down[l], fg,
                  final_norm=(l == depth - 1))
    return xt.reshape(batch, seq, D_MODEL)
```

```python
import functools

import jax
import jax.numpy as jnp
from jax import lax
from jax.experimental import pallas as pl
from jax.experimental.pallas import tpu as pltpu

D_MODEL = 1024
D_FF = 2816
D_CONV = D_MODEL
CONV_WIDTH = 31
HEAD_DIM = 64
HALF = HEAD_DIM // 2
N_HEADS = D_MODEL // HEAD_DIM
N_KV_HEADS = 4
GROUP = N_HEADS // N_KV_HEADS
D_Q = N_HEADS * HEAD_DIM
D_KV = N_KV_HEADS * HEAD_DIM
WINDOW = 128
ROPE_THETA = 10000.0
EPS = 1e-6
LN_EPS = 1e-5
NEG_INF = -1e30
LOG2E = 1.4426950408889634

_SPLITS = (D_CONV, D_CONV, D_Q, D_KV, D_KV, D_MODEL, D_MODEL)
_OFF = tuple(sum(_SPLITS[:i]) for i in range(len(_SPLITS) + 1))
D_IN = _OFF[-1]
D_QKV = D_Q + 2 * D_KV

LANES = 128
SUBLANES = 8
MXU_TILE = 256
V7X_VMEM_BYTES = 64 * 1024 * 1024

FFN_ROWS = 1024
MIX_ROWS = 512
PART_ROWS = 256
CONV_HALO = 32
CONV_CHUNK = 64
ROW_STRIDE = 4
FF_CHUNK = 4 * MXU_TILE
N_SLABS = D_CONV // LANES
STAGE_ELEMS = D_MODEL * D_FF // 16
STAGE_SLOTS = 8
SQUARE_STAGE_ROWS = 128

F32 = jnp.float32
BF16 = jnp.bfloat16

_TN = (((0,), (0,)), ((), ()))


def _rms(x, g):
    return x * lax.rsqrt(jnp.mean(x * x, axis=-1, keepdims=True) + EPS) * g


def _sigmoid(x):
    return 1.0 / (1.0 + jnp.exp2(x * -LOG2E))


def _dot(a, b, dims=None):
    if dims is None:
        return jnp.dot(a, b, preferred_element_type=F32)
    return lax.dot_general(a, b, dims, preferred_element_type=F32)


def _resident(shape):
    return pl.BlockSpec(shape, lambda *_: (0,) * len(shape), pipeline_mode=pl.Buffered(1))


def _fetch_as_bf16(src_hbm, dst_ref, stage_ref, sem_ref):
    n_slots, chunk_rows, _ = stage_ref.shape
    n_chunks = src_hbm.shape[0] // chunk_rows

    def copy(i):
        slot = i % n_slots
        return pltpu.make_async_copy(src_hbm.at[pl.ds(i * chunk_rows, chunk_rows), :], stage_ref.at[slot],
                                     sem_ref.at[slot])

    for i in range(min(n_slots - 1, n_chunks)):
        copy(i).start()
    for i in range(n_chunks):
        if i + n_slots - 1 < n_chunks:
            copy(i + n_slots - 1).start()
        copy(i).wait()
        dst_ref[i * chunk_rows:(i + 1) * chunk_rows, :] = stage_ref[i % n_slots].astype(BF16)


def _ffn_body(x_ref, g_ref, wg_hbm, wu_hbm, wd_hbm, fg_ref, o_ref, wg_ref, wu_ref, wd_ref, stage_in_ref,
              stage_out_ref, sem_ref, *, final_norm):
    @pl.when(pl.program_id(0) == 0)
    def _():
        _fetch_as_bf16(wg_hbm, wg_ref, stage_in_ref, sem_ref.at[0])
        _fetch_as_bf16(wu_hbm, wu_ref, stage_in_ref, sem_ref.at[0])
        _fetch_as_bf16(wd_hbm, wd_ref, stage_out_ref, sem_ref.at[1])

    x = x_ref[...]
    h = _rms(x, g_ref[...]).astype(BF16)
    acc = None
    for c0 in range(0, D_FF, FF_CHUNK):
        c1 = min(c0 + FF_CHUNK, D_FF)
        gate = _dot(h, wg_ref[:, c0:c1])
        up = _dot(h, wu_ref[:, c0:c1])
        a = (gate * _sigmoid(gate) * up).astype(BF16)
        y = _dot(a, wd_ref[c0:c1, :])
        acc = y if acc is None else acc + y
    out = x + 0.5 * acc
    if final_norm:
        out = _rms(out, fg_ref[...])
    o_ref[...] = out


def _ffn(x, norm_g, w_gate, w_up, w_down, final_g, *, final_norm):
    t = x.shape[0]
    rows = pl.BlockSpec((FFN_ROWS, D_MODEL), lambda i: (i, 0))
    weight_bytes = 3 * D_MODEL * D_FF * 2
    stage_bytes = 2 * STAGE_SLOTS * STAGE_ELEMS * 4
    tile_bytes = FFN_ROWS * D_MODEL * 4
    temp_bytes = FFN_ROWS * FF_CHUNK * 4 * 4 + 4 * tile_bytes
    vmem = weight_bytes + stage_bytes + 4 * tile_bytes + temp_bytes
    in_hbm = pl.BlockSpec(memory_space=pl.ANY)
    return pl.pallas_call(
        functools.partial(_ffn_body, final_norm=final_norm),
        grid=(t // FFN_ROWS,),
        in_specs=[rows, _resident((1, D_MODEL)), in_hbm, in_hbm, in_hbm, _resident((1, D_MODEL))],
        out_specs=rows,
        out_shape=jax.ShapeDtypeStruct((t, D_MODEL), F32),
        scratch_shapes=[
            pltpu.VMEM((D_MODEL, D_FF), BF16),
            pltpu.VMEM((D_MODEL, D_FF), BF16),
            pltpu.VMEM((D_FF, D_MODEL), BF16),
            pltpu.VMEM((STAGE_SLOTS, STAGE_ELEMS // D_FF, D_FF), F32),
            pltpu.VMEM((STAGE_SLOTS, STAGE_ELEMS // D_MODEL, D_MODEL), F32),
            pltpu.SemaphoreType.DMA((2, STAGE_SLOTS)),
        ],
        compiler_params=pltpu.CompilerParams(dimension_semantics=("arbitrary",),
                                             vmem_limit_bytes=min(vmem, V7X_VMEM_BYTES)),
        name="ffn_final" if final_norm else "ffn",
    )(x, norm_g, w_gate, w_up, w_down, final_g)


def _causal_conv_chunk(c0, u_ref, dww_ref, dwb_ref, conv_ref):
    per_reg = SUBLANES * ROW_STRIDE
    starts = [c0 + b0 + j for b0 in range(0, CONV_CHUNK, per_reg) for j in range(ROW_STRIDE)]
    for slab in range(N_SLABS):
        l0 = slab * LANES
        accs = [jnp.broadcast_to(dwb_ref[:, l0:l0 + LANES], (SUBLANES, LANES)) for _ in starts]
        for k in range(CONV_WIDTH):
            w_k = dww_ref[k:k + 1, l0:l0 + LANES]
            lag = CONV_WIDTH - 1 - k
            for i, t0 in enumerate(starts):
                tap = u_ref[slab, pl.ds(CONV_HALO + t0 - lag, SUBLANES, stride=ROW_STRIDE), :]
                accs[i] = accs[i] + w_k * tap
        for acc, t0 in zip(accs, starts):
            conv_ref[slab, pl.ds(t0, SUBLANES, stride=ROW_STRIDE), :] = acc


def _mixer_body(x_ref, pos_ref, freq_ref, ng_ref, win_hbm, dww_ref, dwb_ref, lng_ref, lnb_ref,
                wproj_hbm, sink_ref, wo_hbm, gb_ref, wout_hbm, o_ref,
                u_ref, conv_ref, qt_ref, kt_ref, vt_ref, at_ref,
                win_ref, wproj_ref, wo_ref, wout_ref, stage_in_ref, stage_sq_ref, sem_ref):
    first_tile = pl.program_id(1) == 0

    @pl.when(first_tile & (pl.program_id(0) == 0))
    def _():
        _fetch_as_bf16(win_hbm, win_ref, stage_in_ref, sem_ref.at[0])
        for src, dst in ((wproj_hbm, wproj_ref), (wo_hbm, wo_ref), (wout_hbm, wout_ref)):
            _fetch_as_bf16(src, dst, stage_sq_ref, sem_ref.at[1])

    @pl.when(first_tile)
    def _():
        u_ref[:, 0:CONV_HALO, :] = jnp.zeros((N_SLABS, CONV_HALO, LANES), F32)
        kt_ref[:, 0:WINDOW] = jnp.zeros((D_KV, WINDOW), BF16)
        vt_ref[:, 0:WINDOW] = jnp.zeros((D_KV, WINDOW), BF16)

    @pl.when(jnp.logical_not(first_tile))
    def _():
        u_ref[:, 0:CONV_HALO, :] = u_ref[:, MIX_ROWS:MIX_ROWS + CONV_HALO, :]
        kt_ref[:, 0:WINDOW] = kt_ref[:, MIX_ROWS:MIX_ROWS + WINDOW]
        vt_ref[:, 0:WINDOW] = vt_ref[:, MIX_ROWS:MIX_ROWS + WINDOW]

    x = x_ref[...]
    h = _rms(x, ng_ref[...]).astype(BF16)

    def proj(i):
        return _dot(h, win_ref[:, _OFF[i]:_OFF[i + 1]])

    u = proj(0) * _sigmoid(proj(1))
    for slab in range(N_SLABS):
        u_ref[slab, CONV_HALO:, :] = u[:, slab * LANES:(slab + 1) * LANES]

    qkv_t = _dot(h, win_ref[:, _OFF[2]:_OFF[5]]).T
    ang = freq_ref[...] * pos_ref[...].astype(F32)
    cos = jnp.cos(ang)
    sin = jnp.sin(ang)

    def rope_head(r0, c, s):
        x1 = qkv_t[r0:r0 + HALF]
        x2 = qkv_t[r0 + HALF:r0 + HEAD_DIM]
        return (x1 * c - x2 * s).astype(BF16), (x2 * c + x1 * s).astype(BF16)

    q_scale = HEAD_DIM ** -0.5 * LOG2E
    cos_q, sin_q = cos * q_scale, sin * q_scale
    for hd in range(N_HEADS):
        r0 = hd * HEAD_DIM
        qt_ref[r0:r0 + HALF, :], qt_ref[r0 + HALF:r0 + HEAD_DIM, :] = rope_head(r0, cos_q, sin_q)
    for kh in range(N_KV_HEADS):
        r0 = kh * HEAD_DIM
        kt_ref[r0:r0 + HALF, WINDOW:], kt_ref[r0 + HALF:r0 + HEAD_DIM, WINDOW:] = rope_head(D_Q + r0, cos, sin)
    vt_ref[:, WINDOW:] = qkv_t[D_Q + D_KV:D_QKV].astype(BF16)

    gate_c = _sigmoid(proj(5) + gb_ref[:, 0:D_MODEL])
    gate_a = _sigmoid(proj(6) + gb_ref[:, D_MODEL:2 * D_MODEL])

    kc = lax.broadcasted_iota(jnp.int32, (2 * WINDOW, GROUP * WINDOW), 0)
    qq = lax.broadcasted_iota(jnp.int32, (2 * WINDOW, GROUP * WINDOW), 1) % WINDOW
    band = (kc > qq) & (kc <= qq + WINDOW)
    band_first = band & ((kc >= WINDOW) | jnp.logical_not(first_tile))
    bias = jnp.where(band, 0.0, NEG_INF)
    bias_first = jnp.where(band_first, 0.0, NEG_INF)

    def attend(n, kh):
        mask_bias = bias_first if n == 0 else bias
        t0 = n * WINDOW
        f0 = kh * HEAD_DIM
        h0 = kh * GROUP * HEAD_DIM
        k_band = kt_ref[f0:f0 + HEAD_DIM, t0:t0 + 2 * WINDOW]
        v_band = vt_ref[f0:f0 + HEAD_DIM, t0:t0 + 2 * WINDOW]
        q_grp = jnp.concatenate(
            [qt_ref[h0 + g * HEAD_DIM:h0 + (g + 1) * HEAD_DIM, t0:t0 + WINDOW] for g in range(GROUP)],
            axis=1)
        s = _dot(k_band, q_grp, _TN) + mask_bias
        sink = sink_ref[kh] * LOG2E
        m = jnp.maximum(jnp.max(s, axis=0, keepdims=True), sink)
        p = jnp.exp2(s - m)
        inv = 1.0 / (jnp.sum(p, axis=0, keepdims=True) + jnp.exp2(sink - m))
        o_t = _dot(v_band, p.astype(BF16)) * inv
        for g in range(GROUP):
            at_ref[h0 + g * HEAD_DIM:h0 + (g + 1) * HEAD_DIM, t0:t0 + WINDOW] = (
                o_t[:, g * WINDOW:(g + 1) * WINDOW].astype(BF16))

    for r0 in range(0, MIX_ROWS, PART_ROWS):
        rows = slice(r0, r0 + PART_ROWS)
        for c0 in range(r0, r0 + PART_ROWS, CONV_CHUNK):
            _causal_conv_chunk(c0, u_ref, dww_ref, dwb_ref, conv_ref)
        for n in range(r0 // WINDOW, (r0 + PART_ROWS) // WINDOW):
            for kh in range(N_KV_HEADS):
                attend(n, kh)

        cv = jnp.concatenate([conv_ref[slab, rows, :] for slab in range(N_SLABS)], axis=1)
        mu = jnp.mean(cv, axis=-1, keepdims=True)
        cen = cv - mu
        var = jnp.mean(cen * cen, axis=-1, keepdims=True)
        ln = cen * lax.rsqrt(var + LN_EPS) * lng_ref[...] + lnb_ref[...]
        conv_out = _dot((ln * _sigmoid(ln)).astype(BF16), wproj_ref[...])

        attn_out = _dot(at_ref[:, rows], wo_ref[...], _TN)

        merged = (gate_c[rows] * conv_out + gate_a[rows] * attn_out).astype(BF16)
        o_ref[rows, :] = x[rows] + _dot(merged, wout_ref[...])


def _mixer(x, pos, freq, norm_g, w_in, dw_w, dw_b, ln_g, ln_b, w_proj, sinks, w_o, gate_b, w_out, *,
           batch, seq):
    tiles = seq // MIX_ROWS
    rows = pl.BlockSpec((MIX_ROWS, D_MODEL), lambda b, j: (b * tiles + j, 0))
    pos_spec = pl.BlockSpec((None, 1, MIX_ROWS), lambda b, j: (b * tiles + j, 0, 0))
    weight_bytes = (D_MODEL * D_IN + 3 * D_MODEL * D_MODEL) * 2
    stage_bytes = STAGE_SLOTS * (STAGE_ELEMS + SQUARE_STAGE_ROWS * D_MODEL) * 4
    tile_bytes = MIX_ROWS * D_MODEL * 4
    scratch_bytes = ((MIX_ROWS + CONV_HALO) * D_CONV * 4 + tile_bytes
                     + 2 * D_Q * MIX_ROWS * 2 + 2 * D_KV * (MIX_ROWS + WINDOW) * 2)
    temp_bytes = 12 * tile_bytes
    vmem = weight_bytes + stage_bytes + 4 * tile_bytes + scratch_bytes + temp_bytes
    in_hbm = pl.BlockSpec(memory_space=pl.ANY)
    return pl.pallas_call(
        _mixer_body,
        grid=(batch, tiles),
        in_specs=[rows, pos_spec, _resident((HALF, MIX_ROWS)), _resident((1, D_MODEL)),
                  in_hbm,
                  _resident((CONV_WIDTH, D_CONV)), _resident((1, D_CONV)), _resident((1, D_CONV)),
                  _resident((1, D_CONV)), in_hbm,
                  _resident((N_KV_HEADS, 1, GROUP * WINDOW)),
                  in_hbm, _resident((1, 2 * D_MODEL)), in_hbm],
        out_specs=rows,
        out_shape=jax.ShapeDtypeStruct((batch * seq, D_MODEL), F32),
        scratch_shapes=[
            pltpu.VMEM((N_SLABS, CONV_HALO + MIX_ROWS, LANES), F32),
            pltpu.VMEM((N_SLABS, MIX_ROWS, LANES), F32),
            pltpu.VMEM((D_Q, MIX_ROWS), BF16),
            pltpu.VMEM((D_KV, WINDOW + MIX_ROWS), BF16),
            pltpu.VMEM((D_KV, WINDOW + MIX_ROWS), BF16),
            pltpu.VMEM((D_Q, MIX_ROWS), BF16),
            pltpu.VMEM((D_MODEL, D_IN), BF16),
            pltpu.VMEM((D_CONV, D_MODEL), BF16),
            pltpu.VMEM((D_Q, D_MODEL), BF16),
            pltpu.VMEM((D_MODEL, D_MODEL), BF16),
            pltpu.VMEM((STAGE_SLOTS, STAGE_ELEMS // D_IN, D_IN), F32),
            pltpu.VMEM((STAGE_SLOTS, SQUARE_STAGE_ROWS, D_MODEL), F32),
            pltpu.SemaphoreType.DMA((2, STAGE_SLOTS)),
        ],
        compiler_params=pltpu.CompilerParams(dimension_semantics=("arbitrary", "arbitrary"),
                                             vmem_limit_bytes=min(vmem, V7X_VMEM_BYTES)),
        name="mixer",
    )(x, pos, freq, norm_g, w_in, dw_w, dw_b, ln_g, ln_b, w_proj, sinks, w_o, gate_b, w_out)


def kernel(x, positions, ffn1_norm, ffn1_w_gate, ffn1_w_up, ffn1_w_down, mix_norm, w_in, conv_dw_w, conv_dw_b,
           conv_ln_g, conv_ln_b, conv_w_proj, attn_sinks, attn_w_o, gate_b, w_out, ffn2_norm, ffn2_w_gate,
           ffn2_w_up, ffn2_w_down, final_norm):
    batch, seq, d = x.shape
    assert d == D_MODEL and seq % MIX_ROWS == 0 and (batch * seq) % FFN_ROWS == 0 and MIX_ROWS % WINDOW == 0
    depth = ffn1_norm.shape[0]
    t = batch * seq
    xt = x.reshape(t, D_MODEL)
    pos = positions.reshape(t // MIX_ROWS, 1, MIX_ROWS).astype(jnp.int32)
    inv_freq = ROPE_THETA ** (-jnp.arange(HALF, dtype=F32) / HALF)
    freq = jnp.broadcast_to(inv_freq[:, None], (HALF, MIX_ROWS))
    row = lambda a: a.reshape(1, -1)
    fg = row(final_norm)
    for l in range(depth):
        xt = _ffn(xt, row(ffn1_norm[l]), ffn1_w_gate[l], ffn1_w_up[l], ffn1_w_down[l], fg,
                  final_norm=False)
        sinks = jnp.repeat(attn_sinks[l].reshape(N_KV_HEADS, GROUP), WINDOW, axis=1).reshape(
            N_KV_HEADS, 1, GROUP * WINDOW)
        xt = _mixer(xt, pos, freq, row(mix_norm[l]), w_in[l], conv_dw_w[l],
                    row(conv_dw_b[l]), row(conv_ln_g[l]), row(conv_ln_b[l]), conv_w_proj[l], sinks,
                    attn_w_o[l], row(gate_b[l]), w_out[l], batch=batch, seq=seq)
        xt = _ffn(xt, row(ffn2_norm[l]), ffn2_w_gate[l], ffn2_w_up[l], ffn2_w_down[l], fg,
                  final_norm=(l == depth - 1))
    return xt.reshape(batch, seq, D_MODEL)
```

```python
import functools

import jax
import jax.numpy as jnp
from jax import lax
from jax.experimental import pallas as pl
from jax.experimental.pallas import tpu as pltpu

D_MODEL = 1024
D_FF = 2816
D_CONV = D_MODEL
CONV_WIDTH = 31
HEAD_DIM = 64
HALF = HEAD_DIM // 2
N_HEADS = D_MODEL // HEAD_DIM
N_KV_HEADS = 4
GROUP = N_HEADS // N_KV_HEADS
D_Q = N_HEADS * HEAD_DIM
D_KV = N_KV_HEADS * HEAD_DIM
WINDOW = 128
ROPE_THETA = 10000.0
EPS = 1e-6
LN_EPS = 1e-5
NEG_INF = -1e30
LOG2E = 1.4426950408889634

_SPLITS = (D_CONV, D_CONV, D_Q, D_KV, D_KV, D_MODEL, D_MODEL)
_OFF = tuple(sum(_SPLITS[:i]) for i in range(len(_SPLITS) + 1))
D_IN = _OFF[-1]
D_QKV = D_Q + 2 * D_KV

LANES = 128
SUBLANES = 8
MXU_TILE = 256
V7X_VMEM_BYTES = 64 * 1024 * 1024

FFN_ROWS = 1024
MIX_ROWS = 512
PART_ROWS = 256
CONV_HALO = 32
CONV_CHUNK = 64
ROW_STRIDE = 4
FF_CHUNK = 4 * MXU_TILE
N_SLABS = D_CONV // LANES
STAGE_ELEMS = D_MODEL * D_FF // 16
STAGE_SLOTS = 8
SQUARE_STAGE_ROWS = 128

F32 = jnp.float32
BF16 = jnp.bfloat16

_TN = (((0,), (0,)), ((), ()))


def _rms(x, g):
    return x * lax.rsqrt(jnp.mean(x * x, axis=-1, keepdims=True) + EPS) * g


def _sigmoid(x):
    return 1.0 / (1.0 + jnp.exp2(x * -LOG2E))


def _dot(a, b, dims=None):
    if dims is None:
        return jnp.dot(a, b, preferred_element_type=F32)
    return lax.dot_general(a, b, dims, preferred_element_type=F32)


def _resident(shape):
    return pl.BlockSpec(shape, lambda *_: (0,) * len(shape), pipeline_mode=pl.Buffered(1))


def _fetch_as_bf16(src_hbm, dst_ref, stage_ref, sem_ref, ring):
    n_slots, chunk_rows, _ = stage_ref.shape
    n_chunks = src_hbm.shape[0] // chunk_rows

    def copy(i):
        slot = i % n_slots
        return pltpu.make_async_copy(src_hbm.at[pl.ds(i * chunk_rows, chunk_rows), :], stage_ref.at[slot],
                                     sem_ref.at[ring, slot])

    for i in range(min(n_slots - 1, n_chunks)):
        copy(i).start(priority=i % 2)
    for i in range(n_chunks):
        nxt = i + n_slots - 1
        if nxt < n_chunks:
            copy(nxt).start(priority=nxt % 2)
        copy(i).wait()
        dst_ref[i * chunk_rows:(i + 1) * chunk_rows, :] = stage_ref[i % n_slots].astype(BF16)


def _ffn_body(x_ref, g_ref, wg_hbm, wu_hbm, wd_hbm, fg_ref, o_ref, wg_ref, wu_ref, wd_ref, stage_in_ref,
              stage_out_ref, sem_ref, *, final_norm):
    @pl.when(pl.program_id(0) == 0)
    def _():
        _fetch_as_bf16(wg_hbm, wg_ref, stage_in_ref, sem_ref, 0)
        _fetch_as_bf16(wu_hbm, wu_ref, stage_in_ref, sem_ref, 0)
        _fetch_as_bf16(wd_hbm, wd_ref, stage_out_ref, sem_ref, 1)

    x = x_ref[...]
    h = _rms(x, g_ref[...]).astype(BF16)
    acc = None
    for c0 in range(0, D_FF, FF_CHUNK):
        c1 = min(c0 + FF_CHUNK, D_FF)
        gate = _dot(h, wg_ref[:, c0:c1])
        up = _dot(h, wu_ref[:, c0:c1])
        a = (gate * _sigmoid(gate) * up).astype(BF16)
        y = _dot(a, wd_ref[c0:c1, :])
        acc = y if acc is None else acc + y
    out = x + 0.5 * acc
    if final_norm:
        out = _rms(out, fg_ref[...])
    o_ref[...] = out


def _ffn(x, norm_g, w_gate, w_up, w_down, final_g, *, final_norm):
    t = x.shape[0]
    rows = pl.BlockSpec((FFN_ROWS, D_MODEL), lambda i: (i, 0))
    weight_bytes = 3 * D_MODEL * D_FF * 2
    stage_bytes = 2 * STAGE_SLOTS * STAGE_ELEMS * 4
    tile_bytes = FFN_ROWS * D_MODEL * 4
    temp_bytes = FFN_ROWS * FF_CHUNK * 4 * 4 + 4 * tile_bytes
    vmem = weight_bytes + stage_bytes + 4 * tile_bytes + temp_bytes
    in_hbm = pl.BlockSpec(memory_space=pl.ANY)
    return pl.pallas_call(
        functools.partial(_ffn_body, final_norm=final_norm),
        grid=(t // FFN_ROWS,),
        in_specs=[rows, _resident((1, D_MODEL)), in_hbm, in_hbm, in_hbm, _resident((1, D_MODEL))],
        out_specs=rows,
        out_shape=jax.ShapeDtypeStruct((t, D_MODEL), F32),
        scratch_shapes=[
            pltpu.VMEM((D_MODEL, D_FF), BF16),
            pltpu.VMEM((D_MODEL, D_FF), BF16),
            pltpu.VMEM((D_FF, D_MODEL), BF16),
            pltpu.VMEM((STAGE_SLOTS, STAGE_ELEMS // D_FF, D_FF), F32),
            pltpu.VMEM((STAGE_SLOTS, STAGE_ELEMS // D_MODEL, D_MODEL), F32),
            pltpu.SemaphoreType.DMA((2, STAGE_SLOTS)),
        ],
        compiler_params=pltpu.CompilerParams(dimension_semantics=("arbitrary",),
                                             vmem_limit_bytes=min(vmem, V7X_VMEM_BYTES)),
        name="ffn_final" if final_norm else "ffn",
    )(x, norm_g, w_gate, w_up, w_down, final_g)


def _causal_conv_chunk(c0, u_ref, dww_ref, dwb_ref, conv_ref):
    per_reg = SUBLANES * ROW_STRIDE
    starts = [c0 + b0 + j for b0 in range(0, CONV_CHUNK, per_reg) for j in range(ROW_STRIDE)]
    for slab in range(N_SLABS):
        l0 = slab * LANES
        accs = [jnp.broadcast_to(dwb_ref[:, l0:l0 + LANES], (SUBLANES, LANES)) for _ in starts]
        for k in range(CONV_WIDTH):
            w_k = dww_ref[k:k + 1, l0:l0 + LANES]
            lag = CONV_WIDTH - 1 - k
            for i, t0 in enumerate(starts):
                tap = u_ref[slab, pl.ds(CONV_HALO + t0 - lag, SUBLANES, stride=ROW_STRIDE), :]
                accs[i] = accs[i] + w_k * tap
        for acc, t0 in zip(accs, starts):
            conv_ref[slab, pl.ds(t0, SUBLANES, stride=ROW_STRIDE), :] = acc


def _mixer_body(x_ref, pos_ref, freq_ref, ng_ref, win_hbm, dww_ref, dwb_ref, lng_ref, lnb_ref,
                wproj_hbm, sink_ref, wo_hbm, gb_ref, wout_hbm, o_ref,
                u_ref, conv_ref, qt_ref, kt_ref, vt_ref, at_ref,
                win_ref, wproj_ref, wo_ref, wout_ref, stage_in_ref, stage_sq_ref, sem_ref):
    first_tile = pl.program_id(1) == 0

    @pl.when(first_tile & (pl.program_id(0) == 0))
    def _():
        _fetch_as_bf16(win_hbm, win_ref, stage_in_ref, sem_ref, 0)
        for src, dst in ((wproj_hbm, wproj_ref), (wo_hbm, wo_ref), (wout_hbm, wout_ref)):
            _fetch_as_bf16(src, dst, stage_sq_ref, sem_ref, 1)

    @pl.when(first_tile)
    def _():
        u_ref[:, 0:CONV_HALO, :] = jnp.zeros((N_SLABS, CONV_HALO, LANES), F32)
        kt_ref[:, 0:WINDOW] = jnp.zeros((D_KV, WINDOW), BF16)
        vt_ref[:, 0:WINDOW] = jnp.zeros((D_KV, WINDOW), BF16)

    @pl.when(jnp.logical_not(first_tile))
    def _():
        u_ref[:, 0:CONV_HALO, :] = u_ref[:, MIX_ROWS:MIX_ROWS + CONV_HALO, :]
        kt_ref[:, 0:WINDOW] = kt_ref[:, MIX_ROWS:MIX_ROWS + WINDOW]
        vt_ref[:, 0:WINDOW] = vt_ref[:, MIX_ROWS:MIX_ROWS + WINDOW]

    x = x_ref[...]
    h = _rms(x, ng_ref[...]).astype(BF16)

    def proj(i):
        return _dot(h, win_ref[:, _OFF[i]:_OFF[i + 1]])

    u = proj(0) * _sigmoid(proj(1))
    for slab in range(N_SLABS):
        u_ref[slab, CONV_HALO:, :] = u[:, slab * LANES:(slab + 1) * LANES]

    qkv_t = _dot(h, win_ref[:, _OFF[2]:_OFF[5]]).T
    ang = freq_ref[...] * pos_ref[...].astype(F32)
    cos = jnp.cos(ang)
    sin = jnp.sin(ang)

    def rope_head(r0, c, s):
        x1 = qkv_t[r0:r0 + HALF]
        x2 = qkv_t[r0 + HALF:r0 + HEAD_DIM]
        return (x1 * c - x2 * s).astype(BF16), (x2 * c + x1 * s).astype(BF16)

    q_scale = HEAD_DIM ** -0.5 * LOG2E
    cos_q, sin_q = cos * q_scale, sin * q_scale
    for hd in range(N_HEADS):
        r0 = hd * HEAD_DIM
        qt_ref[r0:r0 + HALF, :], qt_ref[r0 + HALF:r0 + HEAD_DIM, :] = rope_head(r0, cos_q, sin_q)
    for kh in range(N_KV_HEADS):
        r0 = kh * HEAD_DIM
        kt_ref[r0:r0 + HALF, WINDOW:], kt_ref[r0 + HALF:r0 + HEAD_DIM, WINDOW:] = rope_head(D_Q + r0, cos, sin)
    vt_ref[:, WINDOW:] = qkv_t[D_Q + D_KV:D_QKV].astype(BF16)

    gate_c = _sigmoid(proj(5) + gb_ref[:, 0:D_MODEL])
    gate_a = _sigmoid(proj(6) + gb_ref[:, D_MODEL:2 * D_MODEL])

    kc = lax.broadcasted_iota(jnp.int32, (2 * WINDOW, GROUP * WINDOW), 0)
    qq = lax.broadcasted_iota(jnp.int32, (2 * WINDOW, GROUP * WINDOW), 1) % WINDOW
    band = (kc > qq) & (kc <= qq + WINDOW)
    band_first = band & ((kc >= WINDOW) | jnp.logical_not(first_tile))
    bias = jnp.where(band, 0.0, NEG_INF)
    bias_first = jnp.where(band_first, 0.0, NEG_INF)

    def attend(n, kh):
        mask_bias = bias_first if n == 0 else bias
        t0 = n * WINDOW
        f0 = kh * HEAD_DIM
        h0 = kh * GROUP * HEAD_DIM
        k_band = kt_ref[f0:f0 + HEAD_DIM, t0:t0 + 2 * WINDOW]
        v_band = vt_ref[f0:f0 + HEAD_DIM, t0:t0 + 2 * WINDOW]
        q_grp = jnp.concatenate(
            [qt_ref[h0 + g * HEAD_DIM:h0 + (g + 1) * HEAD_DIM, t0:t0 + WINDOW] for g in range(GROUP)],
            axis=1)
        s = _dot(k_band, q_grp, _TN) + mask_bias
        sink = sink_ref[kh] * LOG2E
        m = jnp.maximum(jnp.max(s, axis=0, keepdims=True), sink)
        p = jnp.exp2(s - m)
        inv = 1.0 / (jnp.sum(p, axis=0, keepdims=True) + jnp.exp2(sink - m))
        o_t = _dot(v_band, p.astype(BF16)) * inv
        for g in range(GROUP):
            at_ref[h0 + g * HEAD_DIM:h0 + (g + 1) * HEAD_DIM, t0:t0 + WINDOW] = (
                o_t[:, g * WINDOW:(g + 1) * WINDOW].astype(BF16))

    for r0 in range(0, MIX_ROWS, PART_ROWS):
        rows = slice(r0, r0 + PART_ROWS)
        for c0 in range(r0, r0 + PART_ROWS, CONV_CHUNK):
            _causal_conv_chunk(c0, u_ref, dww_ref, dwb_ref, conv_ref)
        for n in range(r0 // WINDOW, (r0 + PART_ROWS) // WINDOW):
            for kh in range(N_KV_HEADS):
                attend(n, kh)

        cv = jnp.concatenate([conv_ref[slab, rows, :] for slab in range(N_SLABS)], axis=1)
        mu = jnp.mean(cv, axis=-1, keepdims=True)
        cen = cv - mu
        var = jnp.mean(cen * cen, axis=-1, keepdims=True)
        ln = cen * lax.rsqrt(var + LN_EPS) * lng_ref[...] + lnb_ref[...]
        conv_out = _dot((ln * _sigmoid(ln)).astype(BF16), wproj_ref[...])

        attn_out = _dot(at_ref[:, rows], wo_ref[...], _TN)

        merged = (gate_c[rows] * conv_out + gate_a[rows] * attn_out).astype(BF16)
        o_ref[rows, :] = x[rows] + _dot(merged, wout_ref[...])


def _mixer(x, pos, freq, norm_g, w_in, dw_w, dw_b, ln_g, ln_b, w_proj, sinks, w_o, gate_b, w_out, *,
           batch, seq):
    tiles = seq // MIX_ROWS
    rows = pl.BlockSpec((MIX_ROWS, D_MODEL), lambda b, j: (b * tiles + j, 0))
    pos_spec = pl.BlockSpec((None, 1, MIX_ROWS), lambda b, j: (b * tiles + j, 0, 0))
    weight_bytes = (D_MODEL * D_IN + 3 * D_MODEL * D_MODEL) * 2
    stage_bytes = STAGE_SLOTS * (STAGE_ELEMS + SQUARE_STAGE_ROWS * D_MODEL) * 4
    tile_bytes = MIX_ROWS * D_MODEL * 4
    scratch_bytes = ((MIX_ROWS + CONV_HALO) * D_CONV * 4 + tile_bytes
                     + 2 * D_Q * MIX_ROWS * 2 + 2 * D_KV * (MIX_ROWS + WINDOW) * 2)
    temp_bytes = 12 * tile_bytes
    vmem = weight_bytes + stage_bytes + 4 * tile_bytes + scratch_bytes + temp_bytes
    in_hbm = pl.BlockSpec(memory_space=pl.ANY)
    return pl.pallas_call(
        _mixer_body,
        grid=(batch, tiles),
        in_specs=[rows, pos_spec, _resident((HALF, MIX_ROWS)), _resident((1, D_MODEL)),
                  in_hbm,
                  _resident((CONV_WIDTH, D_CONV)), _resident((1, D_CONV)), _resident((1, D_CONV)),
                  _resident((1, D_CONV)), in_hbm,
                  _resident((N_KV_HEADS, 1, GROUP * WINDOW)),
                  in_hbm, _resident((1, 2 * D_MODEL)), in_hbm],
        out_specs=rows,
        out_shape=jax.ShapeDtypeStruct((batch * seq, D_MODEL), F32),
        scratch_shapes=[
            pltpu.VMEM((N_SLABS, CONV_HALO + MIX_ROWS, LANES), F32),
            pltpu.VMEM((N_SLABS, MIX_ROWS, LANES), F32),
            pltpu.VMEM((D_Q, MIX_ROWS), BF16),
            pltpu.VMEM((D_KV, WINDOW + MIX_ROWS), BF16),
            pltpu.VMEM((D_KV, WINDOW + MIX_ROWS), BF16),
            pltpu.VMEM((D_Q, MIX_ROWS), BF16),
            pltpu.VMEM((D_MODEL, D_IN), BF16),
            pltpu.VMEM((D_CONV, D_MODEL), BF16),
            pltpu.VMEM((D_Q, D_MODEL), BF16),
            pltpu.VMEM((D_MODEL, D_MODEL), BF16),
            pltpu.VMEM((STAGE_SLOTS, STAGE_ELEMS // D_IN, D_IN), F32),
            pltpu.VMEM((STAGE_SLOTS, SQUARE_STAGE_ROWS, D_MODEL), F32),
            pltpu.SemaphoreType.DMA((2, STAGE_SLOTS)),
        ],
        compiler_params=pltpu.CompilerParams(dimension_semantics=("arbitrary", "arbitrary"),
                                             vmem_limit_bytes=min(vmem, V7X_VMEM_BYTES)),
        name="mixer",
    )(x, pos, freq, norm_g, w_in, dw_w, dw_b, ln_g, ln_b, w_proj, sinks, w_o, gate_b, w_out)


def kernel(x, positions, ffn1_norm, ffn1_w_gate, ffn1_w_up, ffn1_w_down, mix_norm, w_in, conv_dw_w, conv_dw_b,
           conv_ln_g, conv_ln_b, conv_w_proj, attn_sinks, attn_w_o, gate_b, w_out, ffn2_norm, ffn2_w_gate,
           ffn2_w_up, ffn2_w_down, final_norm):
    batch, seq, d = x.shape
    assert d == D_MODEL and seq % MIX_ROWS == 0 and (batch * seq) % FFN_ROWS == 0 and MIX_ROWS % WINDOW == 0
    depth = ffn1_norm.shape[0]
    t = batch * seq
    xt = x.reshape(t, D_MODEL)
    pos = positions.reshape(t // MIX_ROWS, 1, MIX_ROWS).astype(jnp.int32)
    inv_freq = ROPE_THETA ** (-jnp.arange(HALF, dtype=F32) / HALF)
    freq = jnp.broadcast_to(inv_freq[:, None], (HALF, MIX_ROWS))
    row = lambda a: a.reshape(1, -1)
    fg = row(final_norm)
    for l in range(depth):
        xt = _ffn(xt, row(ffn1_norm[l]), ffn1_w_gate[l], ffn1_w_up[l], ffn1_w_down[l], fg,
                  final_norm=False)
        sinks = jnp.repeat(attn_sinks[l].reshape(N_KV_HEADS, GROUP), WINDOW, axis=1).reshape(
            N_KV_HEADS, 1, GROUP * WINDOW)
        xt = _mixer(xt, pos, freq, row(mix_norm[l]), w_in[l], conv_dw_w[l],
                    row(conv_dw_b[l]), row(conv_ln_g[l]), row(conv_ln_b[l]), conv_w_proj[l], sinks,
                    attn_w_o[l], row(gate_b[l]), w_out[l], batch=batch, seq=seq)
        xt = _ffn(xt, row(ffn2_norm[l]), ffn2_w_gate[l], ffn2_w_up[l], ffn2_w_down[l], fg,
                  final_norm=(l == depth - 1))
    return xt.reshape(batch, seq, D_MODEL)
```

```python
import functools

import jax
import jax.numpy as jnp
from jax import lax
from jax.experimental import pallas as pl
from jax.experimental.pallas import tpu as pltpu

D_MODEL = 1024
D_FF = 2816
D_CONV = D_MODEL
CONV_WIDTH = 31
HEAD_DIM = 64
HALF = HEAD_DIM // 2
N_HEADS = D_MODEL // HEAD_DIM
N_KV_HEADS = 4
GROUP = N_HEADS // N_KV_HEADS
D_Q = N_HEADS * HEAD_DIM
D_KV = N_KV_HEADS * HEAD_DIM
WINDOW = 128
ROPE_THETA = 10000.0
EPS = 1e-6
LN_EPS = 1e-5
NEG_INF = -1e30
LOG2E = 1.4426950408889634

_SPLITS = (D_CONV, D_CONV, D_Q, D_KV, D_KV, D_MODEL, D_MODEL)
_OFF = tuple(sum(_SPLITS[:i]) for i in range(len(_SPLITS) + 1))
D_IN = _OFF[-1]
D_QKV = D_Q + 2 * D_KV

LANES = 128
SUBLANES = 8
MXU_TILE = 256
V7X_VMEM_BYTES = 64 * 1024 * 1024

FFN_ROWS = 1024
MIX_ROWS = 512
PART_ROWS = 256
CONV_HALO = 32
CONV_CHUNK = 64
ROW_STRIDE = 4
FF_CHUNK = 4 * MXU_TILE
N_SLABS = D_CONV // LANES
STAGE_ELEMS = D_MODEL * D_FF // 16
STAGE_SLOTS = 8
SQUARE_STAGE_ROWS = 128

F32 = jnp.float32
BF16 = jnp.bfloat16

_TN = (((0,), (0,)), ((), ()))


def _rms(x, g):
    return x * lax.rsqrt(jnp.mean(x * x, axis=-1, keepdims=True) + EPS) * g


def _sigmoid(x):
    return 1.0 / (1.0 + jnp.exp2(x * -LOG2E))


def _dot(a, b, dims=None):
    if dims is None:
        return jnp.dot(a, b, preferred_element_type=F32)
    return lax.dot_general(a, b, dims, preferred_element_type=F32)


def _resident(shape):
    return pl.BlockSpec(shape, lambda *_: (0,) * len(shape), pipeline_mode=pl.Buffered(1))


def _fetch_as_bf16(src_hbm, dst_ref, stage_ref, sem_ref, ring):
    n_slots, chunk_rows, _ = stage_ref.shape
    n_chunks = src_hbm.shape[0] // chunk_rows

    def copy(i):
        slot = i % n_slots
        return pltpu.make_async_copy(src_hbm.at[pl.ds(i * chunk_rows, chunk_rows), :], stage_ref.at[slot],
                                     sem_ref.at[ring, slot])

    for i in range(min(n_slots - 1, n_chunks)):
        copy(i).start(priority=i % 2)
    for i in range(n_chunks):
        nxt = i + n_slots - 1
        if nxt < n_chunks:
            copy(nxt).start(priority=nxt % 2)
        copy(i).wait()
        dst_ref[i * chunk_rows:(i + 1) * chunk_rows, :] = stage_ref[i % n_slots].astype(BF16)


def _ffn_body(x_ref, g_ref, wg_hbm, wu_hbm, wd_hbm, fg_ref, o_ref, wg_ref, wu_ref, wd_ref, stage_in_ref,
              stage_out_ref, sem_ref, *, final_norm):
    @pl.when(pl.program_id(0) == 0)
    def _():
        _fetch_as_bf16(wg_hbm, wg_ref, stage_in_ref, sem_ref, 0)
        _fetch_as_bf16(wu_hbm, wu_ref, stage_in_ref, sem_ref, 0)
        _fetch_as_bf16(wd_hbm, wd_ref, stage_out_ref, sem_ref, 1)

    x = x_ref[...]
    h = _rms(x, g_ref[...]).astype(BF16)
    hidden = []
    for c0 in range(0, D_FF, FF_CHUNK):
        c1 = min(c0 + FF_CHUNK, D_FF)
        gate = _dot(h, wg_ref[:, c0:c1])
        up = _dot(h, wu_ref[:, c0:c1])
        hidden.append((gate * _sigmoid(gate) * up).astype(BF16))
    out = x + 0.5 * _dot(jnp.concatenate(hidden, axis=1), wd_ref[...])
    if final_norm:
        out = _rms(out, fg_ref[...])
    o_ref[...] = out


def _ffn(x, norm_g, w_gate, w_up, w_down, final_g, *, final_norm):
    t = x.shape[0]
    rows = pl.BlockSpec((FFN_ROWS, D_MODEL), lambda i: (i, 0))
    weight_bytes = 3 * D_MODEL * D_FF * 2
    stage_bytes = 2 * STAGE_SLOTS * STAGE_ELEMS * 4
    tile_bytes = FFN_ROWS * D_MODEL * 4
    temp_bytes = FFN_ROWS * FF_CHUNK * 4 * 4 + 4 * tile_bytes
    vmem = weight_bytes + stage_bytes + 4 * tile_bytes + temp_bytes
    in_hbm = pl.BlockSpec(memory_space=pl.ANY)
    return pl.pallas_call(
        functools.partial(_ffn_body, final_norm=final_norm),
        grid=(t // FFN_ROWS,),
        in_specs=[rows, _resident((1, D_MODEL)), in_hbm, in_hbm, in_hbm, _resident((1, D_MODEL))],
        out_specs=rows,
        out_shape=jax.ShapeDtypeStruct((t, D_MODEL), F32),
        scratch_shapes=[
            pltpu.VMEM((D_MODEL, D_FF), BF16),
            pltpu.VMEM((D_MODEL, D_FF), BF16),
            pltpu.VMEM((D_FF, D_MODEL), BF16),
            pltpu.VMEM((STAGE_SLOTS, STAGE_ELEMS // D_FF, D_FF), F32),
            pltpu.VMEM((STAGE_SLOTS, STAGE_ELEMS // D_MODEL, D_MODEL), F32),
            pltpu.SemaphoreType.DMA((2, STAGE_SLOTS)),
        ],
        compiler_params=pltpu.CompilerParams(dimension_semantics=("arbitrary",),
                                             vmem_limit_bytes=min(vmem, V7X_VMEM_BYTES)),
        name="ffn_final" if final_norm else "ffn",
    )(x, norm_g, w_gate, w_up, w_down, final_g)


def _causal_conv_chunk(c0, u_ref, dww_ref, dwb_ref, conv_ref):
    per_reg = SUBLANES * ROW_STRIDE
    starts = [c0 + b0 + j for b0 in range(0, CONV_CHUNK, per_reg) for j in range(ROW_STRIDE)]
    for slab in range(N_SLABS):
        l0 = slab * LANES
        accs = [jnp.broadcast_to(dwb_ref[:, l0:l0 + LANES], (SUBLANES, LANES)) for _ in starts]
        for k in range(CONV_WIDTH):
            w_k = dww_ref[k:k + 1, l0:l0 + LANES]
            lag = CONV_WIDTH - 1 - k
            for i, t0 in enumerate(starts):
                tap = u_ref[slab, pl.ds(CONV_HALO + t0 - lag, SUBLANES, stride=ROW_STRIDE), :]
                accs[i] = accs[i] + w_k * tap
        for acc, t0 in zip(accs, starts):
            conv_ref[slab, pl.ds(t0, SUBLANES, stride=ROW_STRIDE), :] = acc


def _mixer_body(x_ref, pos_ref, freq_ref, ng_ref, win_hbm, dww_ref, dwb_ref, lng_ref, lnb_ref,
                wproj_hbm, sink_ref, wo_hbm, gb_ref, wout_hbm, o_ref,
                u_ref, conv_ref, qt_ref, kt_ref, vt_ref, at_ref,
                win_ref, wproj_ref, wo_ref, wout_ref, stage_in_ref, stage_sq_ref, sem_ref):
    first_tile = pl.program_id(1) == 0

    @pl.when(first_tile & (pl.program_id(0) == 0))
    def _():
        _fetch_as_bf16(win_hbm, win_ref, stage_in_ref, sem_ref, 0)
        for src, dst in ((wproj_hbm, wproj_ref), (wo_hbm, wo_ref), (wout_hbm, wout_ref)):
            _fetch_as_bf16(src, dst, stage_sq_ref, sem_ref, 1)

    @pl.when(first_tile & (pl.program_id(0) == 0))
    def _():
        u_ref[:, MIX_ROWS:, :] = jnp.zeros((N_SLABS, CONV_HALO, LANES), F32)
        kt_ref[:, MIX_ROWS:] = jnp.zeros((D_KV, WINDOW), BF16)
        vt_ref[:, MIX_ROWS:] = jnp.zeros((D_KV, WINDOW), BF16)

    u_ref[:, 0:CONV_HALO, :] = jnp.where(first_tile, 0.0, u_ref[:, MIX_ROWS:MIX_ROWS + CONV_HALO, :])
    kt_ref[:, 0:WINDOW] = jnp.where(first_tile, jnp.zeros((), BF16), kt_ref[:, MIX_ROWS:MIX_ROWS + WINDOW])
    vt_ref[:, 0:WINDOW] = jnp.where(first_tile, jnp.zeros((), BF16), vt_ref[:, MIX_ROWS:MIX_ROWS + WINDOW])

    x = x_ref[...]
    h = _rms(x, ng_ref[...]).astype(BF16)

    def proj(i):
        return _dot(h, win_ref[:, _OFF[i]:_OFF[i + 1]])

    u = proj(0) * _sigmoid(proj(1))
    for slab in range(N_SLABS):
        u_ref[slab, CONV_HALO:, :] = u[:, slab * LANES:(slab + 1) * LANES]

    qkv_t = _dot(h, win_ref[:, _OFF[2]:_OFF[5]]).T
    ang = freq_ref[...] * pos_ref[...].astype(F32)
    cos = jnp.cos(ang)
    sin = jnp.sin(ang)

    def rope_head(r0, c, s):
        x1 = qkv_t[r0:r0 + HALF]
        x2 = qkv_t[r0 + HALF:r0 + HEAD_DIM]
        return (x1 * c - x2 * s).astype(BF16), (x2 * c + x1 * s).astype(BF16)

    q_scale = HEAD_DIM ** -0.5 * LOG2E
    cos_q, sin_q = cos * q_scale, sin * q_scale
    for hd in range(N_HEADS):
        r0 = hd * HEAD_DIM
        qt_ref[r0:r0 + HALF, :], qt_ref[r0 + HALF:r0 + HEAD_DIM, :] = rope_head(r0, cos_q, sin_q)
    for kh in range(N_KV_HEADS):
        r0 = kh * HEAD_DIM
        kt_ref[r0:r0 + HALF, WINDOW:], kt_ref[r0 + HALF:r0 + HEAD_DIM, WINDOW:] = rope_head(D_Q + r0, cos, sin)
    vt_ref[:, WINDOW:] = qkv_t[D_Q + D_KV:D_QKV].astype(BF16)

    gate_c = _sigmoid(proj(5) + gb_ref[:, 0:D_MODEL])
    gate_a = _sigmoid(proj(6) + gb_ref[:, D_MODEL:2 * D_MODEL])

    kc = lax.broadcasted_iota(jnp.int32, (2 * WINDOW, GROUP * WINDOW), 0)
    qq = lax.broadcasted_iota(jnp.int32, (2 * WINDOW, GROUP * WINDOW), 1) % WINDOW
    band = (kc > qq) & (kc <= qq + WINDOW)
    band_first = band & ((kc >= WINDOW) | jnp.logical_not(first_tile))
    bias = jnp.where(band, 0.0, NEG_INF)
    bias_first = jnp.where(band_first, 0.0, NEG_INF)

    def attend(n, kh):
        mask_bias = bias_first if n == 0 else bias
        t0 = n * WINDOW
        f0 = kh * HEAD_DIM
        h0 = kh * GROUP * HEAD_DIM
        k_band = kt_ref[f0:f0 + HEAD_DIM, t0:t0 + 2 * WINDOW]
        v_band = vt_ref[f0:f0 + HEAD_DIM, t0:t0 + 2 * WINDOW]
        q_grp = jnp.concatenate(
            [qt_ref[h0 + g * HEAD_DIM:h0 + (g + 1) * HEAD_DIM, t0:t0 + WINDOW] for g in range(GROUP)],
            axis=1)
        s = _dot(k_band, q_grp, _TN) + mask_bias
        sink = sink_ref[kh] * LOG2E
        m = jnp.maximum(jnp.max(s, axis=0, keepdims=True), sink)
        p = jnp.exp2(s - m)
        inv = 1.0 / (jnp.sum(p, axis=0, keepdims=True) + jnp.exp2(sink - m))
        o_t = _dot(v_band, p.astype(BF16)) * inv
        for g in range(GROUP):
            at_ref[h0 + g * HEAD_DIM:h0 + (g + 1) * HEAD_DIM, t0:t0 + WINDOW] = (
                o_t[:, g * WINDOW:(g + 1) * WINDOW].astype(BF16))

    for r0 in range(0, MIX_ROWS, PART_ROWS):
        rows = slice(r0, r0 + PART_ROWS)
        for c0 in range(r0, r0 + PART_ROWS, CONV_CHUNK):
            _causal_conv_chunk(c0, u_ref, dww_ref, dwb_ref, conv_ref)
        for n in range(r0 // WINDOW, (r0 + PART_ROWS) // WINDOW):
            for kh in range(N_KV_HEADS):
                attend(n, kh)

        cv = jnp.concatenate([conv_ref[slab, rows, :] for slab in range(N_SLABS)], axis=1)
        mu = jnp.mean(cv, axis=-1, keepdims=True)
        cen = cv - mu
        var = jnp.mean(cen * cen, axis=-1, keepdims=True)
        ln = cen * lax.rsqrt(var + LN_EPS) * lng_ref[...] + lnb_ref[...]
        conv_out = _dot((ln * _sigmoid(ln)).astype(BF16), wproj_ref[...])

        attn_out = _dot(at_ref[:, rows], wo_ref[...], _TN)

        merged = (gate_c[rows] * conv_out + gate_a[rows] * attn_out).astype(BF16)
        o_ref[rows, :] = x[rows] + _dot(merged, wout_ref[...])


def _mixer(x, pos, freq, norm_g, w_in, dw_w, dw_b, ln_g, ln_b, w_proj, sinks, w_o, gate_b, w_out, *,
           batch, seq):
    tiles = seq // MIX_ROWS
    rows = pl.BlockSpec((MIX_ROWS, D_MODEL), lambda b, j: (b * tiles + j, 0))
    pos_spec = pl.BlockSpec((None, 1, MIX_ROWS), lambda b, j: (b * tiles + j, 0, 0))
    weight_bytes = (D_MODEL * D_IN + 3 * D_MODEL * D_MODEL) * 2
    stage_bytes = STAGE_SLOTS * (STAGE_ELEMS + SQUARE_STAGE_ROWS * D_MODEL) * 4
    tile_bytes = MIX_ROWS * D_MODEL * 4
    scratch_bytes = ((MIX_ROWS + CONV_HALO) * D_CONV * 4 + tile_bytes
                     + 2 * D_Q * MIX_ROWS * 2 + 2 * D_KV * (MIX_ROWS + WINDOW) * 2)
    temp_bytes = 12 * tile_bytes
    vmem = weight_bytes + stage_bytes + 4 * tile_bytes + scratch_bytes + temp_bytes
    in_hbm = pl.BlockSpec(memory_space=pl.ANY)
    return pl.pallas_call(
        _mixer_body,
        grid=(batch, tiles),
        in_specs=[rows, pos_spec, _resident((HALF, MIX_ROWS)), _resident((1, D_MODEL)),
                  in_hbm,
                  _resident((CONV_WIDTH, D_CONV)), _resident((1, D_CONV)), _resident((1, D_CONV)),
                  _resident((1, D_CONV)), in_hbm,
                  _resident((N_KV_HEADS, 1, GROUP * WINDOW)),
                  in_hbm, _resident((1, 2 * D_MODEL)), in_hbm],
        out_specs=rows,
        out_shape=jax.ShapeDtypeStruct((batch * seq, D_MODEL), F32),
        scratch_shapes=[
            pltpu.VMEM((N_SLABS, CONV_HALO + MIX_ROWS, LANES), F32),
            pltpu.VMEM((N_SLABS, MIX_ROWS, LANES), F32),
            pltpu.VMEM((D_Q, MIX_ROWS), BF16),
            pltpu.VMEM((D_KV, WINDOW + MIX_ROWS), BF16),
            pltpu.VMEM((D_KV, WINDOW + MIX_ROWS), BF16),
            pltpu.VMEM((D_Q, MIX_ROWS), BF16),
            pltpu.VMEM((D_MODEL, D_IN), BF16),
            pltpu.VMEM((D_CONV, D_MODEL), BF16),
            pltpu.VMEM((D_Q, D_MODEL), BF16),
            pltpu.VMEM((D_MODEL, D_MODEL), BF16),
            pltpu.VMEM((STAGE_SLOTS, STAGE_ELEMS // D_IN, D_IN), F32),
            pltpu.VMEM((STAGE_SLOTS, SQUARE_STAGE_ROWS, D_MODEL), F32),
            pltpu.SemaphoreType.DMA((2, STAGE_SLOTS)),
        ],
        compiler_params=pltpu.CompilerParams(dimension_semantics=("arbitrary", "arbitrary"),
                                             vmem_limit_bytes=min(vmem, V7X_VMEM_BYTES)),
        name="mixer",
    )(x, pos, freq, norm_g, w_in, dw_w, dw_b, ln_g, ln_b, w_proj, sinks, w_o, gate_b, w_out)


def kernel(x, positions, ffn1_norm, ffn1_w_gate, ffn1_w_up, ffn1_w_down, mix_norm, w_in, conv_dw_w, conv_dw_b,
           conv_ln_g, conv_ln_b, conv_w_proj, attn_sinks, attn_w_o, gate_b, w_out, ffn2_norm, ffn2_w_gate,
           ffn2_w_up, ffn2_w_down, final_norm):
    batch, seq, d = x.shape
    assert d == D_MODEL and seq % MIX_ROWS == 0 and (batch * seq) % FFN_ROWS == 0 and MIX_ROWS % WINDOW == 0
    depth = ffn1_norm.shape[0]
    t = batch * seq
    xt = x.reshape(t, D_MODEL)
    pos = positions.reshape(t // MIX_ROWS, 1, MIX_ROWS).astype(jnp.int32)
    inv_freq = ROPE_THETA ** (-jnp.arange(HALF, dtype=F32) / HALF)
    freq = jnp.broadcast_to(inv_freq[:, None], (HALF, MIX_ROWS))
    row = lambda a: a.reshape(1, -1)
    fg = row(final_norm)
    for l in range(depth):
        xt = _ffn(xt, row(ffn1_norm[l]), ffn1_w_gate[l], ffn1_w_up[l], ffn1_w_down[l], fg,
                  final_norm=False)
        sinks = jnp.repeat(attn_sinks[l].reshape(N_KV_HEADS, GROUP), WINDOW, axis=1).reshape(
            N_KV_HEADS, 1, GROUP * WINDOW)
        xt = _mixer(xt, pos, freq, row(mix_norm[l]), w_in[l], conv_dw_w[l],
                    row(conv_dw_b[l]), row(conv_ln_g[l]), row(conv_ln_b[l]), conv_w_proj[l], sinks,
                    attn_w_o[l], row(gate_b[l]), w_out[l], batch=batch, seq=seq)
        xt = _ffn(xt, row(ffn2_norm[l]), ffn2_w_gate[l], ffn2_w_up[l], ffn2_w_down[l], fg,
                  final_norm=(l == depth - 1))
    return xt.reshape(batch, seq, D_MODEL)
```
